```python
import jax, jax.numpy as jnp
from jax import lax
import numpy as np

D_MODEL = 1024
BATCH = 4
SEQ = 4096
DEPTH = 1

HEAD_DIM = 64
N_HEADS_SWA = D_MODEL // (2 * HEAD_DIM)
N_KV_SWA = max(1, N_HEADS_SWA // 4)
N_HEADS_FOX = D_MODEL // (2 * HEAD_DIM)
WINDOW = 128
BLOCK = 128
D_FF = ((8 * D_MODEL // 3 + 127) // 128) * 128
CONV_WIDTH = 3
EPS = 1e-6
NEG = -1e30

Q_A = N_HEADS_SWA * HEAD_DIM
KV_A = N_KV_SWA * HEAD_DIM
Q_B = N_HEADS_FOX * HEAD_DIM
F_B = N_HEADS_FOX
D_IN = Q_A + 2 * KV_A + 3 * Q_B + F_B
D_MIX = Q_A + Q_B

kernel_name = "hybrid_swa_sink_alibi_fox_convffn_adaln"


def rms_norm(x, g):
    xf = x.astype(jnp.float32)
    y = xf * lax.rsqrt(jnp.mean(xf * xf, axis=-1, keepdims=True) + EPS)
    return (y * g.astype(jnp.float32)).astype(x.dtype)


def modulate(h, shift, scale):
    return h * (1 + scale[:, None, :]) + shift[:, None, :]


def alibi_slopes(n_heads):
    return jnp.exp2(-8.0 * (jnp.arange(n_heads, dtype=jnp.float32) + 1) / n_heads)


def swa_attention(q, k, v, sinks):
    B, S, H, D = q.shape
    Hkv = k.shape[2]
    G = H // Hkv
    nb = S // BLOCK
    qb = q.reshape(B, nb, BLOCK, Hkv, G, D)

    def band(t):
        prev = jnp.pad(t[:, :S - BLOCK], ((0, 0), (BLOCK, 0), (0, 0), (0, 0)))
        return jnp.concatenate([prev.reshape(B, nb, BLOCK, Hkv, D),
                                t.reshape(B, nb, BLOCK, Hkv, D)], axis=2)

    kb, vb = band(k), band(v)
    scores = jnp.einsum('bnqhgd,bnkhd->bnhgqk', qb, kb).astype(jnp.float32) * (HEAD_DIM ** -0.5)
    n_idx = jnp.arange(nb)[:, None, None]
    q_idx = jnp.arange(BLOCK)[None, :, None]
    k_idx = jnp.arange(2 * BLOCK)[None, None, :]
    dist = (BLOCK + q_idx) - k_idx
    key_abs = n_idx * BLOCK - BLOCK + k_idx
    valid = (dist >= 0) & (dist < WINDOW) & (key_abs >= 0)
    slopes = alibi_slopes(H).reshape(Hkv, G)[:, :, None, None]
    scores = scores - slopes * dist[0].astype(jnp.float32)
    scores = jnp.where(valid[None, :, None, None], scores, NEG)
    sink = sinks.astype(jnp.float32).reshape(Hkv, G)[:, :, None, None]
    m = jnp.maximum(scores.max(axis=-1, keepdims=True), sink)
    p = jnp.exp(scores - m)
    probs = p / (p.sum(axis=-1, keepdims=True) + jnp.exp(sink - m))
    out = jnp.einsum('bnhgqk,bnkhd->bnqhgd', probs.astype(v.dtype), vb)
    return out.reshape(B, S, H, D)


def forgetting_attention(q, k, v, f_logit):
    B, S, H, D = q.shape
    nb = S // BLOCK
    F = jnp.cumsum(jax.nn.log_sigmoid(f_logit.astype(jnp.float32)), axis=1)
    Fk = F.transpose(0, 2, 1)
    qb = q.reshape(B, nb, BLOCK, H, D).transpose(1, 0, 2, 3, 4)
    Fq = Fk.reshape(B, H, nb, BLOCK).transpose(2, 0, 1, 3)
    k_pos = jnp.arange(S)

    def one_block(args):
        qi, Fi, n = args
        s = jnp.einsum('bqhd,bkhd->bhqk', qi, k).astype(jnp.float32) * (HEAD_DIM ** -0.5)
        s = s + (Fi[..., None] - Fk[:, :, None, :])
        q_pos = n * BLOCK + jnp.arange(BLOCK)
        s = jnp.where(k_pos[None, :] <= q_pos[:, None], s, NEG)
        p = jax.nn.softmax(s, axis=-1)
        return jnp.einsum('bhqk,bkhd->bqhd', p.astype(v.dtype), v)

    out = lax.map(one_block, (qb, Fq, jnp.arange(nb)))
    return out.transpose(1, 0, 2, 3, 4).reshape(B, S, H, D)


def causal_dwconv(u, w, b):
    C = u.shape[-1]
    y = lax.conv_general_dilated(u, w[:, None, :], window_strides=(1,),
                                 padding=[(CONV_WIDTH - 1, 0)],
                                 dimension_numbers=('NWC', 'WIO', 'NWC'),
                                 feature_group_count=C)
    return y + b


def hybrid_mixer(h, w_in, b_f, sinks, w_out):
    B, S, _ = h.shape
    z = h @ w_in
    o = 0
    qa = z[..., o:o + Q_A].reshape(B, S, N_HEADS_SWA, HEAD_DIM); o += Q_A
    ka = z[..., o:o + KV_A].reshape(B, S, N_KV_SWA, HEAD_DIM); o += KV_A
    va = z[..., o:o + KV_A].reshape(B, S, N_KV_SWA, HEAD_DIM); o += KV_A
    qb = z[..., o:o + Q_B].reshape(B, S, N_HEADS_FOX, HEAD_DIM); o += Q_B
    kb = z[..., o:o + Q_B].reshape(B, S, N_HEADS_FOX, HEAD_DIM); o += Q_B
    vb = z[..., o:o + Q_B].reshape(B, S, N_HEADS_FOX, HEAD_DIM); o += Q_B
    fb = z[..., o:o + F_B] + b_f
    ya = swa_attention(qa, ka, va, sinks).reshape(B, S, Q_A)
    yb = forgetting_attention(qb, kb, vb, fb).reshape(B, S, Q_B)
    return jnp.concatenate([ya, yb], axis=-1) @ w_out


def conv_glu_mlp(h, w_up, conv_w, conv_b, w_down):
    u = causal_dwconv(h @ w_up, conv_w, conv_b)
    a, g = jnp.split(u, 2, axis=-1)
    return (jax.nn.silu(g) * a) @ w_down


def setup_inputs(seed: int = 0) -> dict:
    key = jax.random.key(seed)
    ks = jax.random.split(key, 16)
    f32 = jnp.float32
    nrm = lambda k, shp, s: jax.random.normal(k, shp, f32) * s
    return {
        "x": nrm(ks[0], (BATCH, SEQ, D_MODEL), 1.0),
        "c": nrm(ks[1], (BATCH, D_MODEL), 1.0),
        "w_ada": nrm(ks[2], (D_MODEL, 6 * D_MODEL), 0.5 * D_MODEL ** -0.5),
        "b_ada": nrm(ks[3], (6 * D_MODEL,), 0.02),
        "g_attn": 1.0 + nrm(ks[4], (D_MODEL,), 0.05),
        "w_in": nrm(ks[5], (D_MODEL, D_IN), D_MODEL ** -0.5),
        "b_f": 2.0 + nrm(ks[6], (N_HEADS_FOX,), 0.5),
        "sinks": nrm(ks[7], (N_HEADS_SWA,), 0.5),
        "w_out": nrm(ks[8], (D_MIX, D_MODEL), D_MIX ** -0.5),
        "g_mlp": 1.0 + nrm(ks[9], (D_MODEL,), 0.05),
        "w_up": nrm(ks[10], (D_MODEL, 2 * D_FF), D_MODEL ** -0.5),
        "conv_w": nrm(ks[11], (CONV_WIDTH, 2 * D_FF), CONV_WIDTH ** -0.5),
        "conv_b": nrm(ks[12], (2 * D_FF,), 0.02),
        "w_down": nrm(ks[13], (D_FF, D_MODEL), D_FF ** -0.5),
        "g_final": 1.0 + nrm(ks[14], (D_MODEL,), 0.05),
    }


def reference(x, c, w_ada, b_ada, g_attn, w_in, b_f, sinks, w_out,
              g_mlp, w_up, conv_w, conv_b, w_down, g_final):
    mod = jax.nn.silu(c) @ w_ada + b_ada
    sh1, sc1, ga1, sh2, sc2, ga2 = jnp.split(mod, 6, axis=-1)
    for _ in range(DEPTH):
        h = modulate(rms_norm(x, g_attn), sh1, sc1)
        x = x + ga1[:, None, :] * hybrid_mixer(h, w_in, b_f, sinks, w_out)
        h2 = modulate(rms_norm(x, g_mlp), sh2, sc2)
        x = x + ga2[:, None, :] * conv_glu_mlp(h2, w_up, conv_w, conv_b, w_down)
    return rms_norm(x, g_final)
```

```python
import functools

import jax
import jax.numpy as jnp
from jax import lax
from jax.experimental import pallas as pl
from jax.experimental.pallas import tpu as pltpu

F32 = jnp.float32
BF16 = jnp.bfloat16

HEAD_DIM = 64
N_HEADS_SWA = 8
N_KV_SWA = 2
N_HEADS_FOX = 8
WINDOW = 128
BLOCK = 128
CONV_WIDTH = 3
EPS = 1e-6
NEG = -1e30
SCALE = HEAD_DIM ** -0.5

LANES = 128
SUBLANES = 8
VMEM_LIMIT = 56 * 1024 * 1024

_NT = (((1,), (1,)), ((), ()))


def _rms(x, g):
    return x * lax.rsqrt(jnp.mean(x * x, axis=-1, keepdims=True) + EPS) * g


def _ada_kernel(ct_ref, w_ref, b_ref, o_ref, sb_ref, *, nb, tn):
    @pl.when(pl.program_id(0) == 0)
    def _():
        ct = ct_ref[...]
        s = ct * jax.nn.sigmoid(ct)
        for b in range(nb):
            sb_ref[b] = jnp.broadcast_to(s[:, b:b + 1], sb_ref.shape[1:])

    for b in range(nb):
        sb = sb_ref[b]
        for cchunk in range(tn // LANES):
            sl = slice(cchunk * LANES, (cchunk + 1) * LANES)
            o_ref[b:b + 1, sl] = jnp.sum(w_ref[:, sl] * sb, axis=0, keepdims=True) + b_ref[:, sl]


def _ada(c, w_ada, b_ada):
    nb, d = c.shape
    n = w_ada.shape[1]
    tn = 512
    return pl.pallas_call(
        functools.partial(_ada_kernel, nb=nb, tn=tn),
        grid=(n // tn,),
        in_specs=[
            pl.BlockSpec((d, nb), lambda j: (0, 0)),
            pl.BlockSpec((d, tn), lambda j: (0, j)),
            pl.BlockSpec((1, tn), lambda j: (0, j)),
        ],
        out_specs=pl.BlockSpec((nb, tn), lambda j: (0, j)),
        out_shape=jax.ShapeDtypeStruct((nb, n), F32),
        scratch_shapes=[pltpu.VMEM((nb, d, LANES), F32)],
        compiler_params=pltpu.CompilerParams(dimension_semantics=("arbitrary",)),
        name="ada",
    )(c.T, w_ada, b_ada.reshape(1, n))


def _prefix_lanes(v):
    lane = lax.broadcasted_iota(jnp.int32, v.shape, 1)
    for d in (1, 2, 4, 8, 16, 32, 64):
        v = v + jnp.where(lane >= d, pltpu.roll(v, d, axis=1), 0.0)
    return v


def _inproj_kernel(x_ref, mod_ref, g_ref, w_ref, bf_ref,
                   qa_ref, ka_ref, va_ref, qb_ref, kb_ref, vb_ref, ft_ref,
                   carry_ref, *, tm, offs):
    x = x_ref[0]
    sh = mod_ref[0, 0:1, :]
    sc = mod_ref[0, 1:2, :]
    h = _rms(x, g_ref[...]) * (1.0 + sc) + sh
    z = jnp.dot(h.astype(BF16), w_ref[...], preferred_element_type=F32)

    o_qa, o_ka, o_va, o_qb, o_kb, o_vb, o_f, o_end = offs
    qa_ref[0] = (z[:, o_qa:o_ka] * SCALE).astype(BF16)
    ka_ref[0] = z[:, o_ka:o_va].astype(BF16)
    va_ref[0] = z[:, o_va:o_qb].astype(BF16)
    qb_ref[0] = (z[:, o_qb:o_kb] * SCALE).astype(BF16)
    kb_ref[0] = z[:, o_kb:o_vb].astype(BF16)
    vb_ref[0] = z[:, o_vb:o_f].astype(BF16)

    f = z[:, o_f:o_end] + bf_ref[...]
    lf = -(jnp.maximum(-f, 0.0) + jnp.log1p(jnp.exp(-jnp.abs(f))))
    lft = lf.T[0:N_HEADS_FOX, :]

    @pl.when(pl.program_id(1) == 0)
    def _():
        carry_ref[...] = jnp.zeros_like(carry_ref)

    carry = carry_ref[...]
    for cchunk in range(tm // LANES):
        sl = slice(cchunk * LANES, (cchunk + 1) * LANES)
        p = _prefix_lanes(lft[:, sl]) + carry
        ft_ref[0, :, sl] = p
        carry = jnp.broadcast_to(p[:, LANES - 1:LANES], carry.shape)
    carry_ref[...] = carry


def _inproj(x, mod3, g_attn, w_ext, bf_pad, offs, tm):
    b, s, d = x.shape
    n = w_ext.shape[1]
    o_qa, o_ka, o_va, o_qb, o_kb, o_vb, o_f, o_end = offs

    def tok(width):
        return pl.BlockSpec((1, tm, width), lambda bi, j: (bi, j, 0))

    widths = (o_ka - o_qa, o_va - o_ka, o_qb - o_va, o_kb - o_qb, o_vb - o_kb, o_f - o_vb)
    out_shape = [jax.ShapeDtypeStruct((b, s, w), BF16) for w in widths]
    out_shape.append(jax.ShapeDtypeStruct((b, N_HEADS_FOX, s), F32))
    out_specs = [tok(w) for w in widths]
    out_specs.append(pl.BlockSpec((1, N_HEADS_FOX, tm), lambda bi, j: (bi, 0, j)))
    return pl.pallas_call(
        functools.partial(_inproj_kernel, tm=tm, offs=offs),
        grid=(b, s // tm),
        in_specs=[
            tok(d),
            pl.BlockSpec((1, 6, d), lambda bi, j: (bi, 0, 0)),
            pl.BlockSpec((1, d), lambda bi, j: (0, 0)),
            pl.BlockSpec((d, n), lambda bi, j: (0, 0)),
            pl.BlockSpec((1, LANES), lambda bi, j: (0, 0)),
        ],
        out_specs=out_specs,
        out_shape=out_shape,
        scratch_shapes=[pltpu.VMEM((N_HEADS_FOX, LANES), F32)],
        compiler_params=pltpu.CompilerParams(
            dimension_semantics=("arbitrary", "arbitrary"),
            vmem_limit_bytes=VMEM_LIMIT),
        name="inproj",
    )(x, mod3, g_attn.reshape(1, d), w_ext, bf_pad)


def _swa_kernel(sink_ref, q_ref, kc_ref, kp_ref, vc_ref, vp_ref, o_ref, *, tq):
    min_key = jnp.where(pl.program_id(1) == 0, BLOCK, 0)
    lane = lax.broadcasted_iota(jnp.int32, (BLOCK, LANES), 1)
    lo = lane < HEAD_DIM
    qi = lax.broadcasted_iota(jnp.int32, (BLOCK, 2 * BLOCK), 0)
    kj = lax.broadcasted_iota(jnp.int32, (BLOCK, 2 * BLOCK), 1)
    dist_i = (BLOCK + qi) - kj
    dist = dist_i.astype(F32)
    band = (dist_i >= 0) & (dist_i < WINDOW)
    n_pairs = N_HEADS_SWA // 2
    group = N_HEADS_SWA // N_KV_SWA

    for r in range(tq // BLOCK):
        rows = slice(r * BLOCK, (r + 1) * BLOCK)
        if r == 0:
            k_prev, v_prev = kp_ref[0], vp_ref[0]
            valid = band & (kj >= min_key)
        else:
            prev = slice((r - 1) * BLOCK, r * BLOCK)
            k_prev, v_prev = kc_ref[0, prev, :], vc_ref[0, prev, :]
            valid = band
        k_all = jnp.concatenate([k_prev, kc_ref[0, rows, :]], axis=0)
        v_all = jnp.concatenate([v_prev, vc_ref[0, rows, :]], axis=0)
        for p in range(n_pairs):
            q = q_ref[0, rows, p * LANES:(p + 1) * LANES]
            kvh = (2 * p) // group
            k2 = k_all[:, kvh * LANES:(kvh + 1) * LANES]
            v2 = v_all[:, kvh * LANES:(kvh + 1) * LANES]
            outs = []
            for e in range(2):
                hd = 2 * p + e
                slope = 2.0 ** (-8.0 * (hd + 1) / N_HEADS_SWA)
                qm = jnp.where(lo if e == 0 else jnp.logical_not(lo), q, jnp.zeros_like(q))
                sc = lax.dot_general(qm, k2, _NT, preferred_element_type=F32)
                sc = jnp.where(valid, sc - slope * dist, NEG)
                sink = sink_ref[hd]
                m = jnp.maximum(jnp.max(sc, axis=1, keepdims=True), sink)
                pr = jnp.exp(sc - m)
                den = jnp.sum(pr, axis=1, keepdims=True) + jnp.exp(sink - m)
                o = jnp.dot(pr.astype(BF16), v2, preferred_element_type=F32)
                outs.append(o / den)
            o_ref[0, rows, p * LANES:(p + 1) * LANES] = jnp.where(lo, outs[0], outs[1]).astype(BF16)


def _swa(qa, ka2, va2, sinks, tq):
    b, s, dq = qa.shape
    dk = ka2.shape[2]
    nsub = tq // BLOCK
    cur = lambda w: pl.BlockSpec((1, tq, w), lambda bi, i: (bi, i, 0))
    prev = lambda w: pl.BlockSpec((1, BLOCK, w), lambda bi, i: (bi, jnp.maximum(i * nsub - 1, 0), 0))
    return pl.pallas_call(
        functools.partial(_swa_kernel, tq=tq),
        grid=(b, s // tq),
        in_specs=[
            pl.BlockSpec(memory_space=pltpu.SMEM),
            cur(dq), cur(dk), prev(dk), cur(dk), prev(dk),
        ],
        out_specs=cur(dq),
        out_shape=jax.ShapeDtypeStruct((b, s, dq), BF16),
        compiler_params=pltpu.CompilerParams(
            dimension_semantics=("arbitrary", "arbitrary"),
            vmem_limit_bytes=VMEM_LIMIT),
        name="swa",
    )(sinks.astype(F32), qa, ka2, ka2, va2, va2)


def _fox_kernel(q_ref, k_ref, v_ref, ft_ref, o_ref, *, tq):
    tk = tq
    i = pl.program_id(2)
    q = q_ref[0]
    lane = lax.broadcasted_iota(jnp.int32, (tq, LANES), 1)
    lo = lane < HEAD_DIM
    zq = jnp.zeros_like(q)
    qs = (jnp.where(lo, q, zq), jnp.where(lo, zq, q))
    last = pl.multiple_of(i * tq + (tq - LANES), LANES)
    fbase = [ft_ref[0, 0, e:e + 1, pl.ds(last, LANES)][:, LANES - 1:LANES] for e in range(2)]
    row = lax.broadcasted_iota(jnp.int32, (tq, tk), 0)
    col = lax.broadcasted_iota(jnp.int32, (tq, tk), 1)
    causal = col <= row

    def step(j, carry, masked):
        start = pl.multiple_of(j * tk, tk)
        ks = k_ref[0, pl.ds(start, tk), :]
        vs = v_ref[0, pl.ds(start, tk), :]
        new = []
        for e in range(2):
            m, l, acc = carry[e]
            s = lax.dot_general(qs[e], ks, _NT, preferred_element_type=F32)
            fk = ft_ref[0, 0, e:e + 1, pl.ds(start, tk)]
            s = s + (fbase[e] - fk)
            if masked:
                s = jnp.where(causal, s, NEG)
            m_new = jnp.maximum(m, jnp.max(s, axis=1, keepdims=True))
            alpha = jnp.exp(m - m_new)
            p = jnp.exp(s - m_new)
            l = alpha * l + jnp.sum(p, axis=1, keepdims=True)
            acc = alpha * acc + jnp.dot(p.astype(BF16), vs, preferred_element_type=F32)
            new.append((m_new, l, acc))
        return tuple(new)

    init = tuple((jnp.full((tq, 1), NEG, F32), jnp.zeros((tq, 1), F32),
                  jnp.zeros((tq, LANES), F32)) for _ in range(2))
    carry = lax.fori_loop(0, i, lambda j, c: step(j, c, False), init)
    carry = step(i, carry, True)
    outs = [acc / l for (_, l, acc) in carry]
    o_ref[0] = jnp.where(lo, outs[0], outs[1]).astype(BF16)


def _fox(qb, kb, vb, ft, tq):
    b, s, dq = qb.shape
    n_pairs = dq // LANES
    return pl.pallas_call(
        functools.partial(_fox_kernel, tq=tq),
        grid=(b, n_pairs, s // tq),
        in_specs=[
            pl.BlockSpec((1, tq, LANES), lambda bi, hp, i: (bi, i, hp)),
            pl.BlockSpec((1, s, LANES), lambda bi, hp, i: (bi, 0, hp)),
            pl.BlockSpec((1, s, LANES), lambda bi, hp, i: (bi, 0, hp)),
            pl.BlockSpec((1, 1, 2, s), lambda bi, hp, i: (bi, hp, 0, 0)),
        ],
        out_specs=pl.BlockSpec((1, tq, LANES), lambda bi, hp, i: (bi, i, hp)),
        out_shape=jax.ShapeDtypeStruct((b, s, dq), BF16),
        compiler_params=pltpu.CompilerParams(
            dimension_semantics=("arbitrary", "arbitrary", "arbitrary"),
            vmem_limit_bytes=VMEM_LIMIT),
        name="fox",
    )(qb, kb, vb, ft)


def _shift_rows(u, prev, k):
    r = pltpu.roll(u, k, axis=0)
    top_idx = lax.broadcasted_iota(jnp.int32, (SUBLANES, u.shape[1]), 0)
    top = jnp.where(top_idx < k, pltpu.roll(prev, k, axis=0), r[0:SUBLANES])
    return jnp.concatenate([top, r[SUBLANES:]], axis=0)


def _causal_conv(u, prev, w, bias):
    return (w[0:1] * _shift_rows(u, prev, 2) + w[1:2] * _shift_rows(u, prev, 1)
            + w[2:3] * u + bias)


def _mlp_kernel(x_ref, ya_ref, yb_ref, mod_ref, gm_ref, gf_ref, woa_ref, wob_ref,
                wua_ref, wug_ref, cwa_ref, cwg_ref, cba_ref, cbg_ref, wd_ref,
                o_ref, pa_ref, pg_ref, h2_ref, x1_ref, acc_ref, *, tm, n_chunks):
    @pl.when(pl.program_id(1) == 0)
    def _():
        pa_ref[...] = jnp.zeros_like(pa_ref)
        pg_ref[...] = jnp.zeros_like(pg_ref)

    ga1 = mod_ref[0, 2:3, :]
    sh2 = mod_ref[0, 3:4, :]
    sc2 = mod_ref[0, 4:5, :]
    ga2 = mod_ref[0, 5:6, :]
    attn = (jnp.dot(ya_ref[0], woa_ref[...], preferred_element_type=F32)
            + jnp.dot(yb_ref[0], wob_ref[...], preferred_element_type=F32))
    x1 = x_ref[0] + ga1 * attn
    h2_ref[...] = (_rms(x1, gm_ref[...]) * (1.0 + sc2) + sh2).astype(BF16)
    x1_ref[...] = x1
    acc_ref[...] = jnp.zeros_like(acc_ref)

    def chunk(ci, _):
        h2 = h2_ref[...]
        ua = jnp.dot(h2, wua_ref[ci], preferred_element_type=F32)
        ug = jnp.dot(h2, wug_ref[ci], preferred_element_type=F32)
        ca = _causal_conv(ua, pa_ref[ci], cwa_ref[ci], cba_ref[ci])
        cg = _causal_conv(ug, pg_ref[ci], cwg_ref[ci], cbg_ref[ci])
        pa_ref[ci] = ua[tm - SUBLANES:tm]
        pg_ref[ci] = ug[tm - SUBLANES:tm]
        act = (cg * jax.nn.sigmoid(cg) * ca).astype(BF16)
        acc_ref[...] += jnp.dot(act, wd_ref[ci], preferred_element_type=F32)
        return 0

    lax.fori_loop(0, n_chunks, chunk, 0)
    o_ref[0] = _rms(x1_ref[...] + ga2 * acc_ref[...], gf_ref[...])


def _mlp(x, ya, yb, mod3, g_mlp, g_final, woa, wob, wua, wug, cwa, cwg, cba, cbg, wd, tm):
    b, s, d = x.shape
    n_chunks, _, tf = wua.shape
    tok = lambda w: pl.BlockSpec((1, tm, w), lambda bi, j: (bi, j, 0))

    def const(shape):
        nd = len(shape)
        return pl.BlockSpec(shape, lambda bi, j: (0,) * nd, pipeline_mode=pl.Buffered(1))

    return pl.pallas_call(
        functools.partial(_mlp_kernel, tm=tm, n_chunks=n_chunks),
        grid=(b, s // tm),
        in_specs=[
            tok(d), tok(ya.shape[2]), tok(yb.shape[2]),
            pl.BlockSpec((1, 6, d), lambda bi, j: (bi, 0, 0)),
            const((1, d)), const((1, d)),
            const(woa.shape), const(wob.shape),
            const(wua.shape), const(wug.shape),
            const(cwa.shape), const(cwg.shape), const(cba.shape), const(cbg.shape),
            const(wd.shape),
        ],
        out_specs=tok(d),
        out_shape=jax.ShapeDtypeStruct((b, s, d), F32),
        scratch_shapes=[
            pltpu.VMEM((n_chunks, SUBLANES, tf), F32),
            pltpu.VMEM((n_chunks, SUBLANES, tf), F32),
            pltpu.VMEM((tm, d), BF16),
            pltpu.VMEM((tm, d), F32),
            pltpu.VMEM((tm, d), F32),
        ],
        compiler_params=pltpu.CompilerParams(
            dimension_semantics=("arbitrary", "arbitrary"),
            vmem_limit_bytes=VMEM_LIMIT),
        name="mlp",
    )(x, ya, yb, mod3, g_mlp.reshape(1, d), g_final.reshape(1, d),
      woa, wob, wua, wug, cwa, cwg, cba, cbg, wd)


def kernel(x, c, w_ada, b_ada, g_attn, w_in, b_f, sinks, w_out, g_mlp, w_up, conv_w, conv_b,
           w_down, g_final):
    b, s, d = x.shape
    q_a = N_HEADS_SWA * HEAD_DIM
    kv_a = N_KV_SWA * HEAD_DIM
    q_b = N_HEADS_FOX * HEAD_DIM
    d_ff = w_down.shape[0]
    assert w_in.shape == (d, q_a + 2 * kv_a + 3 * q_b + N_HEADS_FOX)
    assert d == q_a + q_b and w_up.shape == (d, 2 * d_ff)

    mod3 = _ada(c, w_ada, b_ada).reshape(b, 6, d)

    o = 0
    w_qa = w_in[:, o:o + q_a]; o += q_a
    w_ka = w_in[:, o:o + kv_a]; o += kv_a
    w_va = w_in[:, o:o + kv_a]; o += kv_a
    w_rest = w_in[:, o:o + 3 * q_b]; o += 3 * q_b
    w_f = w_in[:, o:]

    def dup(w):
        w = w.reshape(d, N_KV_SWA, 1, HEAD_DIM)
        return jnp.broadcast_to(w, (d, N_KV_SWA, 2, HEAD_DIM)).reshape(d, 2 * kv_a)

    w_ext = jnp.concatenate(
        [w_qa, dup(w_ka), dup(w_va), w_rest,
         jnp.pad(w_f, ((0, 0), (0, LANES - N_HEADS_FOX)))], axis=1).astype(BF16)
    offs = (0, q_a, q_a + 2 * kv_a, q_a + 4 * kv_a, q_a + 4 * kv_a + q_b,
            q_a + 4 * kv_a + 2 * q_b, q_a + 4 * kv_a + 3 * q_b,
            q_a + 4 * kv_a + 3 * q_b + LANES)
    bf_pad = jnp.pad(b_f.astype(F32), (0, LANES - N_HEADS_FOX)).reshape(1, LANES)

    qa, ka2, va2, qb, kb, vb, ft = _inproj(x, mod3, g_attn, w_ext, bf_pad, offs, tm=512)
    ya = _swa(qa, ka2, va2, sinks, tq=512)
    yb = _fox(qb, kb, vb, ft.reshape(b, N_HEADS_FOX // 2, 2, s), tq=256)

    tf = 256
    n_chunks = d_ff // tf
    chunked = lambda w: w.reshape(w.shape[0], n_chunks, tf).transpose(1, 0, 2)
    wua = chunked(w_up[:, :d_ff]).astype(BF16)
    wug = chunked(w_up[:, d_ff:]).astype(BF16)
    cwa = chunked(conv_w[:, :d_ff])
    cwg = chunked(conv_w[:, d_ff:])
    cba = chunked(conv_b[None, :d_ff])
    cbg = chunked(conv_b[None, d_ff:])
    wd = w_down.reshape(n_chunks, tf, d).astype(BF16)
    woa = w_out[:q_a].astype(BF16)
    wob = w_out[q_a:].astype(BF16)
    return _mlp(x, ya, yb, mod3, g_mlp, g_final, woa, wob, wua, wug, cwa, cwg, cba, cbg, wd,
                tm=512)
```

```python
import functools

import jax
import jax.numpy as jnp
from jax import lax
from jax.experimental import pallas as pl
from jax.experimental.pallas import tpu as pltpu

F32 = jnp.float32
BF16 = jnp.bfloat16

HEAD_DIM = 64
N_HEADS_SWA = 8
N_KV_SWA = 2
N_HEADS_FOX = 8
WINDOW = 128
BLOCK = 128
CONV_WIDTH = 3
EPS = 1e-6
NEG = -1e30
SCALE = HEAD_DIM ** -0.5

LANES = 128
SUBLANES = 8
VMEM_LIMIT = 56 * 1024 * 1024
MLP_ROW_BLOCKS = 4

_NT = (((1,), (1,)), ((), ()))


def _rms(x, g):
    return x * lax.rsqrt(jnp.mean(x * x, axis=-1, keepdims=True) + EPS) * g


def _ada_kernel(ct_ref, w_ref, b_ref, o_ref, sb_ref, *, nb, tn):
    @pl.when(pl.program_id(0) == 0)
    def _():
        ct = ct_ref[...]
        s = ct * jax.nn.sigmoid(ct)
        for b in range(nb):
            sb_ref[b] = jnp.broadcast_to(s[:, b:b + 1], sb_ref.shape[1:])

    for b in range(nb):
        sb = sb_ref[b]
        for cchunk in range(tn // LANES):
            sl = slice(cchunk * LANES, (cchunk + 1) * LANES)
            o_ref[b:b + 1, sl] = jnp.sum(w_ref[:, sl] * sb, axis=0, keepdims=True) + b_ref[:, sl]


def _ada(c, w_ada, b_ada):
    nb, d = c.shape
    n = w_ada.shape[1]
    tn = 512
    return pl.pallas_call(
        functools.partial(_ada_kernel, nb=nb, tn=tn),
        grid=(n // tn,),
        in_specs=[
            pl.BlockSpec((d, nb), lambda j: (0, 0)),
            pl.BlockSpec((d, tn), lambda j: (0, j)),
            pl.BlockSpec((1, tn), lambda j: (0, j)),
        ],
        out_specs=pl.BlockSpec((nb, tn), lambda j: (0, j)),
        out_shape=jax.ShapeDtypeStruct((nb, n), F32),
        scratch_shapes=[pltpu.VMEM((nb, d, LANES), F32)],
        compiler_params=pltpu.CompilerParams(dimension_semantics=("arbitrary",)),
        name="ada",
    )(c.T, w_ada, b_ada.reshape(1, n))


def _prefix_rows(v):
    row = lax.broadcasted_iota(jnp.int32, v.shape, 0)
    d = 1
    while d < v.shape[0]:
        v = v + jnp.where(row >= d, pltpu.roll(v, d, axis=0), 0.0)
        d *= 2
    return v


def _inproj_kernel(x_ref, mod_ref, g_ref, w_ref, bf_ref,
                   qa_ref, ka_ref, va_ref, qf_ref, kf_ref, vf_ref,
                   carry_ref, *, tm, offs):
    x = x_ref[0]
    sh = mod_ref[0, 0:1, :]
    sc = mod_ref[0, 1:2, :]
    h = (_rms(x, g_ref[...]) * (1.0 + sc) + sh).astype(BF16)

    def proj(first, last):
        return jnp.dot(h, w_ref[:, offs[first][0]:offs[last][1]], preferred_element_type=F32)

    f = proj("f", "f") + bf_ref[...]
    lf = -(jnp.maximum(-f, 0.0) + jnp.log1p(jnp.exp(-jnp.abs(f))))
    za = proj("qa", "va")
    base = offs["qa"][0]
    for name, ref in (("qa", qa_ref), ("ka", ka_ref), ("va", va_ref)):
        ref[0] = za[:, offs[name][0] - base:offs[name][1] - base].astype(BF16)

    @pl.when(pl.program_id(1) == 0)
    def _():
        carry_ref[...] = jnp.zeros_like(carry_ref)

    fcum = _prefix_rows(lf) + carry_ref[0:1, :]
    carry_ref[...] = jnp.broadcast_to(fcum[tm - 1:tm, :], carry_ref.shape)

    lane = lax.broadcasted_iota(jnp.int32, (tm, LANES), 1)
    low = lane < HEAD_DIM

    def spare_lanes(hd):
        odd = hd % 2
        return (jnp.logical_not(low) if odd else low), (0 if odd else HEAD_DIM)

    def k_spare(hd):
        _, a0 = spare_lanes(hd)
        nf = -jnp.broadcast_to(fcum[:, hd:hd + 1], (tm, LANES))
        hi = nf.astype(BF16).astype(F32)
        rem = nf - hi
        mid = rem.astype(BF16).astype(F32)
        return jnp.where(lane == a0, hi,
                         jnp.where(lane == a0 + 1, mid,
                                   jnp.where(lane == a0 + 2, rem - mid, 0.0)))

    def q_spare(hd):
        _, a0 = spare_lanes(hd)
        return jnp.where((lane >= a0) & (lane < a0 + 3), 1.0, 0.0)

    def v_spare(hd):
        _, a0 = spare_lanes(hd)
        return jnp.where(lane == a0, 1.0, 0.0)

    for name, ref, spare in (("qb", qf_ref, q_spare), ("kb", kf_ref, k_spare),
                             ("vb", vf_ref, v_spare)):
        z = proj(name, name)
        for hd in range(N_HEADS_FOX):
            pair = hd // 2
            data, _ = spare_lanes(hd)
            ref[0, :, hd * LANES:(hd + 1) * LANES] = jnp.where(
                data, z[:, pair * LANES:(pair + 1) * LANES], spare(hd)).astype(BF16)


def _inproj(x, mod3, g_attn, w_ext, bf_pad, offs, tm):
    b, s, d = x.shape
    n = w_ext.shape[1]

    def tok(width):
        return pl.BlockSpec((1, tm, width), lambda bi, j: (bi, j, 0))

    fox_width = N_HEADS_FOX * LANES
    widths = tuple(offs[k][1] - offs[k][0] for k in ("qa", "ka", "va")) + (fox_width,) * 3
    out_shape = [jax.ShapeDtypeStruct((b, s, w), BF16) for w in widths]
    out_specs = [tok(w) for w in widths]
    return pl.pallas_call(
        functools.partial(_inproj_kernel, tm=tm, offs=offs),
        grid=(b, s // tm),
        in_specs=[
            tok(d),
            pl.BlockSpec((1, 6, d), lambda bi, j: (bi, 0, 0)),
            pl.BlockSpec((1, d), lambda bi, j: (0, 0)),
            pl.BlockSpec((d, n), lambda bi, j: (0, 0)),
            pl.BlockSpec((1, LANES), lambda bi, j: (0, 0)),
        ],
        out_specs=out_specs,
        out_shape=out_shape,
        scratch_shapes=[pltpu.VMEM((N_HEADS_FOX, LANES), F32)],
        compiler_params=pltpu.CompilerParams(
            dimension_semantics=("arbitrary", "arbitrary"),
            vmem_limit_bytes=VMEM_LIMIT),
        name="inproj",
    )(x, mod3, g_attn.reshape(1, d), w_ext, bf_pad)


def _swa_kernel(sink_ref, q_ref, kc_ref, kp_ref, vc_ref, vp_ref, o_ref, *, tq):
    min_key = jnp.where(pl.program_id(1) == 0, BLOCK, 0)
    lane = lax.broadcasted_iota(jnp.int32, (BLOCK, LANES), 1)
    lo = lane < HEAD_DIM
    qi = lax.broadcasted_iota(jnp.int32, (BLOCK, 2 * BLOCK), 0)
    kj = lax.broadcasted_iota(jnp.int32, (BLOCK, 2 * BLOCK), 1)
    dist_i = (BLOCK + qi) - kj
    dist = dist_i.astype(F32)
    band = (dist_i >= 0) & (dist_i < WINDOW)
    n_pairs = N_HEADS_SWA // 2
    group = N_HEADS_SWA // N_KV_SWA

    for r in range(tq // BLOCK):
        rows = slice(r * BLOCK, (r + 1) * BLOCK)
        if r == 0:
            k_prev, v_prev = kp_ref[0], vp_ref[0]
            valid = band & (kj >= min_key)
        else:
            prev = slice((r - 1) * BLOCK, r * BLOCK)
            k_prev, v_prev = kc_ref[0, prev, :], vc_ref[0, prev, :]
            valid = band
        k_all = jnp.concatenate([k_prev, kc_ref[0, rows, :]], axis=0)
        v_all = jnp.concatenate([v_prev, vc_ref[0, rows, :]], axis=0)
        for p in range(n_pairs):
            q = q_ref[0, rows, p * LANES:(p + 1) * LANES]
            kvh = (2 * p) // group
            k2 = k_all[:, kvh * LANES:(kvh + 1) * LANES]
            v2 = v_all[:, kvh * LANES:(kvh + 1) * LANES]
            outs = []
            for e in range(2):
                hd = 2 * p + e
                slope = 2.0 ** (-8.0 * (hd + 1) / N_HEADS_SWA)
                qm = jnp.where(lo if e == 0 else jnp.logical_not(lo), q, jnp.zeros_like(q))
                sc = lax.dot_general(qm, k2, _NT, preferred_element_type=F32)
                sc = jnp.where(valid, sc - slope * dist, NEG)
                sink = sink_ref[hd]
                m = jnp.maximum(jnp.max(sc, axis=1, keepdims=True), sink)
                pr = jnp.exp(sc - m)
                den = jnp.sum(pr, axis=1, keepdims=True) + jnp.exp(sink - m)
                o = jnp.dot(pr.astype(BF16), v2, preferred_element_type=F32)
                outs.append(o / den)
            o_ref[0, rows, p * LANES:(p + 1) * LANES] = jnp.where(lo, outs[0], outs[1]).astype(BF16)


def _swa(qa, ka2, va2, sinks, tq):
    b, s, dq = qa.shape
    dk = ka2.shape[2]
    nsub = tq // BLOCK
    cur = lambda w: pl.BlockSpec((1, tq, w), lambda bi, i: (bi, i, 0))
    prev = lambda w: pl.BlockSpec((1, BLOCK, w), lambda bi, i: (bi, jnp.maximum(i * nsub - 1, 0), 0))
    return pl.pallas_call(
        functools.partial(_swa_kernel, tq=tq),
        grid=(b, s // tq),
        in_specs=[
            pl.BlockSpec(memory_space=pltpu.SMEM),
            cur(dq), cur(dk), prev(dk), cur(dk), prev(dk),
        ],
        out_specs=cur(dq),
        out_shape=jax.ShapeDtypeStruct((b, s, dq), BF16),
        compiler_params=pltpu.CompilerParams(
            dimension_semantics=("arbitrary", "arbitrary"),
            vmem_limit_bytes=VMEM_LIMIT),
        name="swa",
    )(sinks.astype(F32), qa, ka2, ka2, va2, va2)


def _fox_kernel(q_ref, k_ref, v_ref, o_ref, *, tq, tk):
    i = pl.program_id(2)
    heads = range(2)
    blk = [slice(e * LANES, (e + 1) * LANES) for e in heads]
    qs = [q_ref[0, :, blk[e]] for e in heads]

    def step(start, width, carry, masked):
        new = []
        for e in heads:
            m, acc = carry[e]
            ks = k_ref[0, pl.ds(start, width), blk[e]]
            vs = v_ref[0, pl.ds(start, width), blk[e]]
            s = lax.dot_general(qs[e], ks, _NT, preferred_element_type=F32)
            if masked:
                row = lax.broadcasted_iota(jnp.int32, (tq, width), 0)
                col = lax.broadcasted_iota(jnp.int32, (tq, width), 1)
                s = jnp.where(col <= row, s, NEG)
            m_new = jnp.maximum(m, jnp.max(s, axis=1, keepdims=True))
            p = jnp.exp(s - m_new).astype(BF16)
            acc = jnp.exp(m - m_new) * acc + jnp.dot(p, vs, preferred_element_type=F32)
            new.append((m_new, acc))
        return tuple(new)

    init = tuple((jnp.full((tq, 1), NEG, F32), jnp.zeros((tq, LANES), F32)) for _ in heads)
    n_keys = i * tq
    n_wide = n_keys // tk
    carry = lax.fori_loop(
        0, n_wide, lambda j, c: step(pl.multiple_of(j * tk, tk), tk, c, False), init)
    if tk != tq:
        n_narrow = (n_keys - n_wide * tk) // tq
        carry = lax.fori_loop(
            0, n_narrow,
            lambda j, c: step(pl.multiple_of(n_wide * tk + j * tq, tq), tq, c, False), carry)
    carry = step(pl.multiple_of(n_keys, tq), tq, carry, True)

    (_, acc_e), (_, acc_o) = carry
    lane = lax.broadcasted_iota(jnp.int32, (tq, LANES), 1)
    o_ref[0] = jnp.where(lane < HEAD_DIM,
                         acc_e / acc_e[:, HEAD_DIM:HEAD_DIM + 1],
                         acc_o / acc_o[:, 0:1]).astype(BF16)


def _fox(qf, kf, vf, tq, tk):
    b, s, width = qf.shape
    n_pairs = width // (2 * LANES)
    pair = lambda rows, idx: pl.BlockSpec((1, rows, 2 * LANES), idx)
    return pl.pallas_call(
        functools.partial(_fox_kernel, tq=tq, tk=tk),
        grid=(b, n_pairs, s // tq),
        in_specs=[
            pair(tq, lambda bi, hp, i: (bi, i, hp)),
            pair(s, lambda bi, hp, i: (bi, 0, hp)),
            pair(s, lambda bi, hp, i: (bi, 0, hp)),
        ],
        out_specs=pl.BlockSpec((1, tq, LANES), lambda bi, hp, i: (bi, i, hp)),
        out_shape=jax.ShapeDtypeStruct((b, s, n_pairs * LANES), BF16),
        compiler_params=pltpu.CompilerParams(
            dimension_semantics=("arbitrary", "arbitrary", "arbitrary"),
            vmem_limit_bytes=VMEM_LIMIT),
        name="fox",
    )(qf, kf, vf)


def _shift_rows(u, prev, k):
    r = pltpu.roll(u, k, axis=0)
    top_idx = lax.broadcasted_iota(jnp.int32, (SUBLANES, u.shape[1]), 0)
    top = jnp.where(top_idx < k, pltpu.roll(prev, k, axis=0), r[0:SUBLANES])
    return jnp.concatenate([top, r[SUBLANES:]], axis=0)


def _causal_conv(u, prev, w, bias):
    return (w[0:1] * _shift_rows(u, prev, 2) + w[1:2] * _shift_rows(u, prev, 1)
            + w[2:3] * u + bias)


def _mlp_kernel(x_ref, ya_ref, yb_ref, mod_ref, gm_ref, gf_ref, woa_ref, wob_ref,
                wua_ref, wug_ref, cwa_ref, cwg_ref, cba_ref, cbg_ref, wd_ref,
                o_ref, pa_ref, pg_ref, h2_ref, x1_ref, acc_ref, *, tm, n_chunks):
    @pl.when(pl.program_id(1) == 0)
    def _():
        pa_ref[...] = jnp.zeros_like(pa_ref)
        pg_ref[...] = jnp.zeros_like(pg_ref)

    ga1 = mod_ref[0, 2:3, :]
    sh2 = mod_ref[0, 3:4, :]
    sc2 = mod_ref[0, 4:5, :]
    ga2 = mod_ref[0, 5:6, :]
    rb = tm // MLP_ROW_BLOCKS
    blocks = [slice(r * rb, (r + 1) * rb) for r in range(MLP_ROW_BLOCKS)]

    for rows in blocks:
        attn = (jnp.dot(ya_ref[0, rows, :], woa_ref[...], preferred_element_type=F32)
                + jnp.dot(yb_ref[0, rows, :], wob_ref[...], preferred_element_type=F32))
        x1 = x_ref[0, rows, :] + ga1 * attn
        h2_ref[rows, :] = (_rms(x1, gm_ref[...]) * (1.0 + sc2) + sh2).astype(BF16)
        x1_ref[rows, :] = x1
    acc_ref[...] = jnp.zeros_like(acc_ref)

    def chunk(ci, _):
        ups = []
        for rows in blocks:
            h2 = h2_ref[rows, :]
            ups.append((jnp.dot(h2, wua_ref[ci], preferred_element_type=F32),
                        jnp.dot(h2, wug_ref[ci], preferred_element_type=F32)))
        for r, rows in enumerate(blocks):
            ua, ug = ups[r]
            prev_a = pa_ref[ci] if r == 0 else ups[r - 1][0][rb - SUBLANES:rb]
            prev_g = pg_ref[ci] if r == 0 else ups[r - 1][1][rb - SUBLANES:rb]
            ca = _causal_conv(ua, prev_a, cwa_ref[ci], cba_ref[ci])
            cg = _causal_conv(ug, prev_g, cwg_ref[ci], cbg_ref[ci])
            act = (cg * jax.nn.sigmoid(cg) * ca).astype(BF16)
            acc_ref[rows, :] += jnp.dot(act, wd_ref[ci], preferred_element_type=F32)
        pa_ref[ci] = ups[-1][0][rb - SUBLANES:rb]
        pg_ref[ci] = ups[-1][1][rb - SUBLANES:rb]
        return 0

    lax.fori_loop(0, n_chunks, chunk, 0)
    for rows in blocks:
        o_ref[0, rows, :] = _rms(x1_ref[rows, :] + ga2 * acc_ref[rows, :], gf_ref[...])


def _mlp(x, ya, yb, mod3, g_mlp, g_final, woa, wob, wua, wug, cwa, cwg, cba, cbg, wd, tm):
    b, s, d = x.shape
    n_chunks, _, tf = wua.shape
    tok = lambda w: pl.BlockSpec((1, tm, w), lambda bi, j: (bi, j, 0))

    def const(shape):
        nd = len(shape)
        return pl.BlockSpec(shape, lambda bi, j: (0,) * nd, pipeline_mode=pl.Buffered(1))

    return pl.pallas_call(
        functools.partial(_mlp_kernel, tm=tm, n_chunks=n_chunks),
        grid=(b, s // tm),
        in_specs=[
            tok(d), tok(ya.shape[2]), tok(yb.shape[2]),
            pl.BlockSpec((1, 6, d), lambda bi, j: (bi, 0, 0)),
            const((1, d)), const((1, d)),
            const(woa.shape), const(wob.shape),
            const(wua.shape), const(wug.shape),
            const(cwa.shape), const(cwg.shape), const(cba.shape), const(cbg.shape),
            const(wd.shape),
        ],
        out_specs=tok(d),
        out_shape=jax.ShapeDtypeStruct((b, s, d), F32),
        scratch_shapes=[
            pltpu.VMEM((n_chunks, SUBLANES, tf), F32),
            pltpu.VMEM((n_chunks, SUBLANES, tf), F32),
            pltpu.VMEM((tm, d), BF16),
            pltpu.VMEM((tm, d), F32),
            pltpu.VMEM((tm, d), F32),
        ],
        compiler_params=pltpu.CompilerParams(
            dimension_semantics=("arbitrary", "arbitrary"),
            vmem_limit_bytes=VMEM_LIMIT),
        name="mlp",
    )(x, ya, yb, mod3, g_mlp.reshape(1, d), g_final.reshape(1, d),
      woa, wob, wua, wug, cwa, cwg, cba, cbg, wd)


def kernel(x, c, w_ada, b_ada, g_attn, w_in, b_f, sinks, w_out, g_mlp, w_up, conv_w, conv_b,
           w_down, g_final):
    b, s, d = x.shape
    q_a = N_HEADS_SWA * HEAD_DIM
    kv_a = N_KV_SWA * HEAD_DIM
    q_b = N_HEADS_FOX * HEAD_DIM
    d_ff = w_down.shape[0]
    assert w_in.shape == (d, q_a + 2 * kv_a + 3 * q_b + N_HEADS_FOX)
    assert d == q_a + q_b and w_up.shape == (d, 2 * d_ff)

    mod3 = _ada(c, w_ada, b_ada).reshape(b, 6, d)

    o = 0
    w_qa = w_in[:, o:o + q_a]; o += q_a
    w_ka = w_in[:, o:o + kv_a]; o += kv_a
    w_va = w_in[:, o:o + kv_a]; o += kv_a
    w_rest = w_in[:, o:o + 3 * q_b]; o += 3 * q_b
    w_f = w_in[:, o:]

    def dup(w):
        w = w.reshape(d, N_KV_SWA, 1, HEAD_DIM)
        return jnp.broadcast_to(w, (d, N_KV_SWA, 2, HEAD_DIM)).reshape(d, 2 * kv_a)

    parts = [("f", jnp.pad(w_f, ((0, 0), (0, LANES - N_HEADS_FOX)))),
             ("qa", w_qa * SCALE), ("ka", dup(w_ka)), ("va", dup(w_va)),
             ("qb", w_rest[:, :q_b] * SCALE), ("kb", w_rest[:, q_b:2 * q_b]),
             ("vb", w_rest[:, 2 * q_b:])]
    w_ext = jnp.concatenate([w for _, w in parts], axis=1).astype(BF16)
    offs, o = {}, 0
    for name, w in parts:
        offs[name] = (o, o + w.shape[1])
        o += w.shape[1]
    bf_pad = jnp.pad(b_f.astype(F32), (0, LANES - N_HEADS_FOX)).reshape(1, LANES)

    qa, ka2, va2, qf, kf, vf = _inproj(x, mod3, g_attn, w_ext, bf_pad, offs, tm=512)
    ya = _swa(qa, ka2, va2, sinks, tq=512)
    yb = _fox(qf, kf, vf, tq=1024, tk=1024)

    tf = 256
    n_chunks = d_ff // tf
    chunked = lambda w: w.reshape(w.shape[0], n_chunks, tf).transpose(1, 0, 2)
    wua = chunked(w_up[:, :d_ff]).astype(BF16)
    wug = chunked(w_up[:, d_ff:]).astype(BF16)
    cwa = chunked(conv_w[:, :d_ff])
    cwg = chunked(conv_w[:, d_ff:])
    cba = chunked(conv_b[None, :d_ff])
    cbg = chunked(conv_b[None, d_ff:])
    wd = w_down.reshape(n_chunks, tf, d).astype(BF16)
    woa = w_out[:q_a].astype(BF16)
    wob = w_out[q_a:].astype(BF16)
    return _mlp(x, ya, yb, mod3, g_mlp, g_final, woa, wob, wua, wug, cwa, cwg, cba, cbg, wd,
                tm=512)
```

```python
import functools

import jax
import jax.numpy as jnp
from jax import lax
from jax.experimental import pallas as pl
from jax.experimental.pallas import tpu as pltpu

F32 = jnp.float32
BF16 = jnp.bfloat16

HEAD_DIM = 64
N_HEADS_SWA = 8
N_KV_SWA = 2
N_HEADS_FOX = 8
WINDOW = 128
BLOCK = 128
CONV_WIDTH = 3
EPS = 1e-6
NEG = -1e30
SCALE = HEAD_DIM ** -0.5

LANES = 128
SUBLANES = 8
VMEM_LIMIT = 56 * 1024 * 1024

_NT = (((1,), (1,)), ((), ()))


def _rms(x, g):
    return x * lax.rsqrt(jnp.mean(x * x, axis=-1, keepdims=True) + EPS) * g


def _ada_kernel(ct_ref, w_ref, b_ref, o_ref, sb_ref, *, nb, tn):
    @pl.when(pl.program_id(0) == 0)
    def _():
        ct = ct_ref[...]
        s = ct * jax.nn.sigmoid(ct)
        for b in range(nb):
            sb_ref[b] = jnp.broadcast_to(s[:, b:b + 1], sb_ref.shape[1:])

    for b in range(nb):
        sb = sb_ref[b]
        for cchunk in range(tn // LANES):
            sl = slice(cchunk * LANES, (cchunk + 1) * LANES)
            o_ref[b:b + 1, sl] = jnp.sum(w_ref[:, sl] * sb, axis=0, keepdims=True) + b_ref[:, sl]


def _ada(c, w_ada, b_ada):
    nb, d = c.shape
    n = w_ada.shape[1]
    tn = 512
    return pl.pallas_call(
        functools.partial(_ada_kernel, nb=nb, tn=tn),
        grid=(n // tn,),
        in_specs=[
            pl.BlockSpec((d, nb), lambda j: (0, 0)),
            pl.BlockSpec((d, tn), lambda j: (0, j)),
            pl.BlockSpec((1, tn), lambda j: (0, j)),
        ],
        out_specs=pl.BlockSpec((nb, tn), lambda j: (0, j)),
        out_shape=jax.ShapeDtypeStruct((nb, n), F32),
        scratch_shapes=[pltpu.VMEM((nb, d, LANES), F32)],
        compiler_params=pltpu.CompilerParams(dimension_semantics=("arbitrary",)),
        name="ada",
    )(c.T, w_ada, b_ada.reshape(1, n))


def _prefix_rows(v):
    row = lax.broadcasted_iota(jnp.int32, v.shape, 0)
    d = 1
    while d < v.shape[0]:
        v = v + jnp.where(row >= d, pltpu.roll(v, d, axis=0), 0.0)
        d *= 2
    return v


def _inproj_kernel(x_ref, mod_ref, g_ref, w_ref, bf_ref,
                   qa_ref, ka_ref, va_ref, qf_ref, kf_ref, vf_ref,
                   carry_ref, *, tm, offs):
    x = x_ref[0]
    sh = mod_ref[0, 0:1, :]
    sc = mod_ref[0, 1:2, :]
    h = (_rms(x, g_ref[...]) * (1.0 + sc) + sh).astype(BF16)

    def proj(first, last):
        return jnp.dot(h, w_ref[:, offs[first][0]:offs[last][1]], preferred_element_type=F32)

    f = proj("f", "f") + bf_ref[...]
    lf = -(jnp.maximum(-f, 0.0) + jnp.log1p(jnp.exp(-jnp.abs(f))))
    za = proj("qa", "va")
    base = offs["qa"][0]
    for name, ref in (("qa", qa_ref), ("ka", ka_ref), ("va", va_ref)):
        ref[0] = za[:, offs[name][0] - base:offs[name][1] - base].astype(BF16)

    @pl.when(pl.program_id(1) == 0)
    def _():
        carry_ref[...] = jnp.zeros_like(carry_ref)

    fcum = _prefix_rows(lf) + carry_ref[0:1, :]
    carry_ref[...] = jnp.broadcast_to(fcum[tm - 1:tm, :], carry_ref.shape)

    lane = lax.broadcasted_iota(jnp.int32, (tm, LANES), 1)
    low = lane < HEAD_DIM

    def spare_lanes(hd):
        odd = hd % 2
        return (jnp.logical_not(low) if odd else low), (0 if odd else HEAD_DIM)

    def k_spare(hd):
        _, a0 = spare_lanes(hd)
        nf = -jnp.broadcast_to(fcum[:, hd:hd + 1], (tm, LANES))
        hi = nf.astype(BF16).astype(F32)
        rem = nf - hi
        mid = rem.astype(BF16).astype(F32)
        return jnp.where(lane == a0, hi,
                         jnp.where(lane == a0 + 1, mid,
                                   jnp.where(lane == a0 + 2, rem - mid, 0.0)))

    def q_spare(hd):
        _, a0 = spare_lanes(hd)
        return jnp.where((lane >= a0) & (lane < a0 + 3), 1.0, 0.0)

    def v_spare(hd):
        _, a0 = spare_lanes(hd)
        return jnp.where(lane == a0, 1.0, 0.0)

    for name, ref, spare in (("qb", qf_ref, q_spare), ("kb", kf_ref, k_spare),
                             ("vb", vf_ref, v_spare)):
        z = proj(name, name)
        for hd in range(N_HEADS_FOX):
            pair = hd // 2
            data, _ = spare_lanes(hd)
            ref[0, :, hd * LANES:(hd + 1) * LANES] = jnp.where(
                data, z[:, pair * LANES:(pair + 1) * LANES], spare(hd)).astype(BF16)


def _inproj(x, mod3, g_attn, w_ext, bf_pad, offs, tm):
    b, s, d = x.shape
    n = w_ext.shape[1]

    def tok(width):
        return pl.BlockSpec((1, tm, width), lambda bi, j: (bi, j, 0))

    fox_width = N_HEADS_FOX * LANES
    widths = tuple(offs[k][1] - offs[k][0] for k in ("qa", "ka", "va")) + (fox_width,) * 3
    out_shape = [jax.ShapeDtypeStruct((b, s, w), BF16) for w in widths]
    out_specs = [tok(w) for w in widths]
    return pl.pallas_call(
        functools.partial(_inproj_kernel, tm=tm, offs=offs),
        grid=(b, s // tm),
        in_specs=[
            tok(d),
            pl.BlockSpec((1, 6, d), lambda bi, j: (bi, 0, 0)),
            pl.BlockSpec((1, d), lambda bi, j: (0, 0)),
            pl.BlockSpec((d, n), lambda bi, j: (0, 0)),
            pl.BlockSpec((1, LANES), lambda bi, j: (0, 0)),
        ],
        out_specs=out_specs,
        out_shape=out_shape,
        scratch_shapes=[pltpu.VMEM((N_HEADS_FOX, LANES), F32)],
        compiler_params=pltpu.CompilerParams(
            dimension_semantics=("arbitrary", "arbitrary"),
            vmem_limit_bytes=VMEM_LIMIT),
        name="inproj",
    )(x, mod3, g_attn.reshape(1, d), w_ext, bf_pad)


def _swa_kernel(sink_ref, q_ref, kc_ref, kp_ref, vc_ref, vp_ref, o_ref, *, tq):
    min_key = jnp.where(pl.program_id(1) == 0, BLOCK, 0)
    lane = lax.broadcasted_iota(jnp.int32, (BLOCK, LANES), 1)
    lo = lane < HEAD_DIM
    qi = lax.broadcasted_iota(jnp.int32, (BLOCK, 2 * BLOCK), 0)
    kj = lax.broadcasted_iota(jnp.int32, (BLOCK, 2 * BLOCK), 1)
    dist_i = (BLOCK + qi) - kj
    dist = dist_i.astype(F32)
    band = (dist_i >= 0) & (dist_i < WINDOW)
    n_pairs = N_HEADS_SWA // 2
    group = N_HEADS_SWA // N_KV_SWA

    for r in range(tq // BLOCK):
        rows = slice(r * BLOCK, (r + 1) * BLOCK)
        if r == 0:
            k_prev, v_prev = kp_ref[0], vp_ref[0]
            valid = band & (kj >= min_key)
        else:
            prev = slice((r - 1) * BLOCK, r * BLOCK)
            k_prev, v_prev = kc_ref[0, prev, :], vc_ref[0, prev, :]
            valid = band
        k_all = jnp.concatenate([k_prev, kc_ref[0, rows, :]], axis=0)
        v_all = jnp.concatenate([v_prev, vc_ref[0, rows, :]], axis=0)
        for p in range(n_pairs):
            q = q_ref[0, rows, p * LANES:(p + 1) * LANES]
            kvh = (2 * p) // group
            k2 = k_all[:, kvh * LANES:(kvh + 1) * LANES]
            v2 = v_all[:, kvh * LANES:(kvh + 1) * LANES]
            outs = []
            for e in range(2):
                hd = 2 * p + e
                slope = 2.0 ** (-8.0 * (hd + 1) / N_HEADS_SWA)
                qm = jnp.where(lo if e == 0 else jnp.logical_not(lo), q, jnp.zeros_like(q))
                sc = lax.dot_general(qm, k2, _NT, preferred_element_type=F32)
                sc = jnp.where(valid, sc - slope * dist, NEG)
                sink = sink_ref[hd]
                m = jnp.maximum(jnp.max(sc, axis=1, keepdims=True), sink)
                pr = jnp.exp(sc - m)
                den = jnp.sum(pr, axis=1, keepdims=True) + jnp.exp(sink - m)
                o = jnp.dot(pr.astype(BF16), v2, preferred_element_type=F32)
                outs.append(o / den)
            o_ref[0, rows, p * LANES:(p + 1) * LANES] = jnp.where(lo, outs[0], outs[1]).astype(BF16)


def _swa(qa, ka2, va2, sinks, tq):
    b, s, dq = qa.shape
    dk = ka2.shape[2]
    nsub = tq // BLOCK
    cur = lambda w: pl.BlockSpec((1, tq, w), lambda bi, i: (bi, i, 0))
    prev = lambda w: pl.BlockSpec((1, BLOCK, w), lambda bi, i: (bi, jnp.maximum(i * nsub - 1, 0), 0))
    return pl.pallas_call(
        functools.partial(_swa_kernel, tq=tq),
        grid=(b, s // tq),
        in_specs=[
            pl.BlockSpec(memory_space=pltpu.SMEM),
            cur(dq), cur(dk), prev(dk), cur(dk), prev(dk),
        ],
        out_specs=cur(dq),
        out_shape=jax.ShapeDtypeStruct((b, s, dq), BF16),
        compiler_params=pltpu.CompilerParams(
            dimension_semantics=("arbitrary", "arbitrary"),
            vmem_limit_bytes=VMEM_LIMIT),
        name="swa",
    )(sinks.astype(F32), qa, ka2, ka2, va2, va2)


def _fox_kernel(q_ref, k_ref, v_ref, o_ref, *, tq, tk):
    i = pl.program_id(2)
    heads = range(2)
    blk = [slice(e * LANES, (e + 1) * LANES) for e in heads]
    qs = [q_ref[0, :, blk[e]] for e in heads]

    def step(start, width, carry, masked):
        new = []
        for e in heads:
            m, acc = carry[e]
            ks = k_ref[0, pl.ds(start, width), blk[e]]
            vs = v_ref[0, pl.ds(start, width), blk[e]]
            s = lax.dot_general(qs[e], ks, _NT, preferred_element_type=F32)
            if masked:
                row = lax.broadcasted_iota(jnp.int32, (tq, width), 0)
                col = lax.broadcasted_iota(jnp.int32, (tq, width), 1)
                s = jnp.where(col <= row, s, NEG)
            m_new = jnp.maximum(m, jnp.max(s, axis=1, keepdims=True))
            p = jnp.exp(s - m_new).astype(BF16)
            acc = jnp.exp(m - m_new) * acc + jnp.dot(p, vs, preferred_element_type=F32)
            new.append((m_new, acc))
        return tuple(new)

    init = tuple((jnp.full((tq, 1), NEG, F32), jnp.zeros((tq, LANES), F32)) for _ in heads)
    n_keys = i * tq
    n_wide = n_keys // tk
    carry = lax.fori_loop(
        0, n_wide, lambda j, c: step(pl.multiple_of(j * tk, tk), tk, c, False), init)
    if tk != tq:
        n_narrow = (n_keys - n_wide * tk) // tq
        carry = lax.fori_loop(
            0, n_narrow,
            lambda j, c: step(pl.multiple_of(n_wide * tk + j * tq, tq), tq, c, False), carry)
    carry = step(pl.multiple_of(n_keys, tq), tq, carry, True)

    (_, acc_e), (_, acc_o) = carry
    lane = lax.broadcasted_iota(jnp.int32, (tq, LANES), 1)
    o_ref[0] = jnp.where(lane < HEAD_DIM,
                         acc_e / acc_e[:, HEAD_DIM:HEAD_DIM + 1],
                         acc_o / acc_o[:, 0:1]).astype(BF16)


def _fox(qf, kf, vf, tq, tk):
    b, s, width = qf.shape
    n_pairs = width // (2 * LANES)
    pair = lambda rows, idx: pl.BlockSpec((1, rows, 2 * LANES), idx)
    return pl.pallas_call(
        functools.partial(_fox_kernel, tq=tq, tk=tk),
        grid=(b, n_pairs, s // tq),
        in_specs=[
            pair(tq, lambda bi, hp, i: (bi, i, hp)),
            pair(s, lambda bi, hp, i: (bi, 0, hp)),
            pair(s, lambda bi, hp, i: (bi, 0, hp)),
        ],
        out_specs=pl.BlockSpec((1, tq, LANES), lambda bi, hp, i: (bi, i, hp)),
        out_shape=jax.ShapeDtypeStruct((b, s, n_pairs * LANES), BF16),
        compiler_params=pltpu.CompilerParams(
            dimension_semantics=("arbitrary", "arbitrary", "arbitrary"),
            vmem_limit_bytes=VMEM_LIMIT),
        name="fox",
    )(qf, kf, vf)


def _shift_rows(u, prev, k):
    r = pltpu.roll(u, k, axis=0)
    top_idx = lax.broadcasted_iota(jnp.int32, (SUBLANES, u.shape[1]), 0)
    top = jnp.where(top_idx < k, pltpu.roll(prev, k, axis=0), r[0:SUBLANES])
    return jnp.concatenate([top, r[SUBLANES:]], axis=0)


def _causal_conv(u, prev, w, bias):
    return (w[0:1] * _shift_rows(u, prev, 2) + w[1:2] * _shift_rows(u, prev, 1)
            + w[2:3] * u + bias)


def _mlp_kernel(x_ref, ya_ref, yb_ref, mod_ref, gm_ref, gf_ref, wo_ref, wu_ref, cw_ref, cb_ref,
                wd_ref, o_ref, prev_ref, h2_ref, x1_ref, *, tm, d_ff, tf):
    @pl.when(pl.program_id(1) == 0)
    def _():
        prev_ref[...] = jnp.zeros_like(prev_ref)

    ga1 = mod_ref[0, 2:3, :]
    sh2 = mod_ref[0, 3:4, :]
    sc2 = mod_ref[0, 4:5, :]
    ga2 = mod_ref[0, 5:6, :]
    q_a = ya_ref.shape[2]
    attn = (jnp.dot(ya_ref[0], wo_ref[0:q_a, :], preferred_element_type=F32)
            + jnp.dot(yb_ref[0], wo_ref[q_a:, :], preferred_element_type=F32))
    x1 = x_ref[0] + ga1 * attn
    h2_ref[...] = (_rms(x1, gm_ref[...]) * (1.0 + sc2) + sh2).astype(BF16)
    x1_ref[...] = x1

    def conv_cols(col0):
        cols = slice(col0, col0 + tf)
        u = jnp.dot(h2_ref[...], wu_ref[:, cols], preferred_element_type=F32)
        y = _causal_conv(u, prev_ref[:, cols], cw_ref[:, cols], cb_ref[:, cols])
        prev_ref[:, cols] = u[tm - SUBLANES:tm]
        return y

    acts = []
    for c in range(d_ff // tf):
        ca = conv_cols(c * tf)
        cg = conv_cols(d_ff + c * tf)
        acts.append((cg * jax.nn.sigmoid(cg) * ca).astype(BF16))
    mlp = None
    for c, act in enumerate(acts):
        part = jnp.dot(act, wd_ref[c * tf:(c + 1) * tf, :], preferred_element_type=F32)
        mlp = part if mlp is None else mlp + part
    o_ref[0] = _rms(x1_ref[...] + ga2 * mlp, gf_ref[...])


def _mlp(x, ya, yb, mod3, g_mlp, g_final, wo, wu, cw, cb, wd, tm, tf):
    b, s, d = x.shape
    d_ff = wd.shape[0]
    tok = lambda w: pl.BlockSpec((1, tm, w), lambda bi, j: (bi, j, 0))

    def const(shape):
        nd = len(shape)
        return pl.BlockSpec(shape, lambda bi, j: (0,) * nd, pipeline_mode=pl.Buffered(1))

    return pl.pallas_call(
        functools.partial(_mlp_kernel, tm=tm, d_ff=d_ff, tf=tf),
        grid=(b, s // tm),
        in_specs=[
            tok(d), tok(ya.shape[2]), tok(yb.shape[2]),
            pl.BlockSpec((1, 6, d), lambda bi, j: (bi, 0, 0)),
            const((1, d)), const((1, d)),
            const(wo.shape), const(wu.shape), const(cw.shape), const(cb.shape), const(wd.shape),
        ],
        out_specs=tok(d),
        out_shape=jax.ShapeDtypeStruct((b, s, d), F32),
        scratch_shapes=[
            pltpu.VMEM((SUBLANES, 2 * d_ff), F32),
            pltpu.VMEM((tm, d), BF16),
            pltpu.VMEM((tm, d), F32),
        ],
        compiler_params=pltpu.CompilerParams(
            dimension_semantics=("arbitrary", "arbitrary"),
            vmem_limit_bytes=VMEM_LIMIT),
        name="mlp",
    )(x, ya, yb, mod3, g_mlp.reshape(1, d), g_final.reshape(1, d), wo, wu, cw, cb, wd)


def _inproj_weights(w_in, b_f):
    d = w_in.shape[0]
    q_a = N_HEADS_SWA * HEAD_DIM
    kv_a = N_KV_SWA * HEAD_DIM
    q_b = N_HEADS_FOX * HEAD_DIM
    assert w_in.shape == (d, q_a + 2 * kv_a + 3 * q_b + N_HEADS_FOX)
    o = 0
    w_qa = w_in[:, o:o + q_a]; o += q_a
    w_ka = w_in[:, o:o + kv_a]; o += kv_a
    w_va = w_in[:, o:o + kv_a]; o += kv_a
    w_rest = w_in[:, o:o + 3 * q_b]; o += 3 * q_b
    w_f = w_in[:, o:]

    def dup(w):
        w = w.reshape(d, N_KV_SWA, 1, HEAD_DIM)
        return jnp.broadcast_to(w, (d, N_KV_SWA, 2, HEAD_DIM)).reshape(d, 2 * kv_a)

    parts = [("f", jnp.pad(w_f, ((0, 0), (0, LANES - N_HEADS_FOX)))),
             ("qa", w_qa * SCALE), ("ka", dup(w_ka)), ("va", dup(w_va)),
             ("qb", w_rest[:, :q_b] * SCALE), ("kb", w_rest[:, q_b:2 * q_b]),
             ("vb", w_rest[:, 2 * q_b:])]
    w_ext = jnp.concatenate([w for _, w in parts], axis=1).astype(BF16)
    offs, o = {}, 0
    for name, w in parts:
        offs[name] = (o, o + w.shape[1])
        o += w.shape[1]
    bf_pad = jnp.pad(b_f.astype(F32), (0, LANES - N_HEADS_FOX)).reshape(1, LANES)
    return w_ext, offs, bf_pad


def kernel(x, c, w_ada, b_ada, g_attn, w_in, b_f, sinks, w_out, g_mlp, w_up, conv_w, conv_b,
           w_down, g_final):
    b, s, d = x.shape
    d_ff = w_down.shape[0]
    assert w_up.shape == (d, 2 * d_ff) and w_out.shape == (d, d)

    mod3 = _ada(c, w_ada, b_ada).reshape(b, 6, d)
    w_ext, offs, bf_pad = _inproj_weights(w_in, b_f)
    qa, ka2, va2, qf, kf, vf = _inproj(x, mod3, g_attn, w_ext, bf_pad, offs, tm=512)
    ya = _swa(qa, ka2, va2, sinks, tq=512)
    yb = _fox(qf, kf, vf, tq=1024, tk=1024)

    return _mlp(x, ya, yb, mod3, g_mlp, g_final, w_out.astype(BF16), w_up.astype(BF16),
                conv_w, conv_b.reshape(1, 2 * d_ff), w_down.astype(BF16), tm=512, tf=256)
```

```python
import functools

import jax
import jax.numpy as jnp
from jax import lax
from jax.experimental import pallas as pl
from jax.experimental.pallas import tpu as pltpu

F32 = jnp.float32
BF16 = jnp.bfloat16

HEAD_DIM = 64
N_HEADS_SWA = 8
N_KV_SWA = 2
N_HEADS_FOX = 8
WINDOW = 128
BLOCK = 128
CONV_WIDTH = 3
EPS = 1e-6
NEG = -1e30
SCALE = HEAD_DIM ** -0.5

LANES = 128
SUBLANES = 8
VMEM_LIMIT = 56 * 1024 * 1024
FOX_DIAG_ROW_BLOCKS = 2

_NT = (((1,), (1,)), ((), ()))


def _rms(x, g):
    return x * lax.rsqrt(jnp.mean(x * x, axis=-1, keepdims=True) + EPS) * g


def _ada_kernel(ct_ref, w_ref, b_ref, o_ref, sb_ref, *, nb, tn):
    @pl.when(pl.program_id(0) == 0)
    def _():
        ct = ct_ref[...]
        s = ct * jax.nn.sigmoid(ct)
        for b in range(nb):
            sb_ref[b] = jnp.broadcast_to(s[:, b:b + 1], sb_ref.shape[1:])

    for cchunk in range(tn // LANES):
        sl = slice(cchunk * LANES, (cchunk + 1) * LANES)
        w = w_ref[:, sl]
        for b in range(nb):
            o_ref[b:b + 1, sl] = jnp.sum(w * sb_ref[b], axis=0, keepdims=True) + b_ref[:, sl]


def _ada(c, w_ada, b_ada):
    nb, d = c.shape
    n = w_ada.shape[1]
    tn = 512
    return pl.pallas_call(
        functools.partial(_ada_kernel, nb=nb, tn=tn),
        grid=(n // tn,),
        in_specs=[
            pl.BlockSpec((d, nb), lambda j: (0, 0)),
            pl.BlockSpec((d, tn), lambda j: (0, j)),
            pl.BlockSpec((1, tn), lambda j: (0, j)),
        ],
        out_specs=pl.BlockSpec((nb, tn), lambda j: (0, j)),
        out_shape=jax.ShapeDtypeStruct((nb, n), F32),
        scratch_shapes=[pltpu.VMEM((nb, d, LANES), F32)],
        compiler_params=pltpu.CompilerParams(dimension_semantics=("arbitrary",)),
        name="ada",
    )(c.T, w_ada, b_ada.reshape(1, n))


def _prefix_rows(v):
    row = lax.broadcasted_iota(jnp.int32, v.shape, 0)
    d = 1
    while d < v.shape[0]:
        v = v + jnp.where(row >= d, pltpu.roll(v, d, axis=0), 0.0)
        d *= 2
    return v


def _inproj_kernel(x_ref, mod_ref, g_ref, w_ref, bf_ref,
                   qa_ref, ka_ref, va_ref, qf_ref, kf_ref, vf_ref,
                   carry_ref, *, tm, offs):
    x = x_ref[0]
    sh = mod_ref[0, 0:1, :]
    sc = mod_ref[0, 1:2, :]
    h = (_rms(x, g_ref[...]) * (1.0 + sc) + sh).astype(BF16)

    def proj(first, last):
        return jnp.dot(h, w_ref[:, offs[first][0]:offs[last][1]], preferred_element_type=F32)

    f = proj("f", "f") + bf_ref[...]
    lf = -(jnp.maximum(-f, 0.0) + jnp.log1p(jnp.exp(-jnp.abs(f))))
    za = proj("qa", "va")
    base = offs["qa"][0]
    for name, ref in (("qa", qa_ref), ("ka", ka_ref), ("va", va_ref)):
        ref[0] = za[:, offs[name][0] - base:offs[name][1] - base].astype(BF16)

    @pl.when(pl.program_id(1) == 0)
    def _():
        carry_ref[...] = jnp.zeros_like(carry_ref)

    fcum = _prefix_rows(lf) + carry_ref[0:1, :]
    carry_ref[...] = jnp.broadcast_to(fcum[tm - 1:tm, :], carry_ref.shape)

    lane = lax.broadcasted_iota(jnp.int32, (tm, LANES), 1)
    low = lane < HEAD_DIM

    def spare_lanes(hd):
        odd = hd % 2
        return (jnp.logical_not(low) if odd else low), (0 if odd else HEAD_DIM)

    def k_spare(hd):
        _, a0 = spare_lanes(hd)
        nf = -jnp.broadcast_to(fcum[:, hd:hd + 1], (tm, LANES))
        hi = nf.astype(BF16).astype(F32)
        rem = nf - hi
        mid = rem.astype(BF16).astype(F32)
        return jnp.where(lane == a0, hi,
                         jnp.where(lane == a0 + 1, mid,
                                   jnp.where(lane == a0 + 2, rem - mid, 0.0)))

    def q_spare(hd):
        _, a0 = spare_lanes(hd)
        return jnp.where((lane >= a0) & (lane < a0 + 3), 1.0, 0.0)

    def v_spare(hd):
        _, a0 = spare_lanes(hd)
        return jnp.where(lane == a0, 1.0, 0.0)

    for name, ref, spare in (("qb", qf_ref, q_spare), ("kb", kf_ref, k_spare),
                             ("vb", vf_ref, v_spare)):
        z = proj(name, name)
        for hd in range(N_HEADS_FOX):
            pair = hd // 2
            data, _ = spare_lanes(hd)
            ref[0, :, hd * LANES:(hd + 1) * LANES] = jnp.where(
                data, z[:, pair * LANES:(pair + 1) * LANES], spare(hd)).astype(BF16)


def _inproj(x, mod3, g_attn, w_ext, bf_pad, offs, tm):
    b, s, d = x.shape
    n = w_ext.shape[1]

    def tok(width):
        return pl.BlockSpec((1, tm, width), lambda bi, j: (bi, j, 0))

    fox_width = N_HEADS_FOX * LANES
    widths = tuple(offs[k][1] - offs[k][0] for k in ("qa", "ka", "va")) + (fox_width,) * 3
    out_shape = [jax.ShapeDtypeStruct((b, s, w), BF16) for w in widths]
    out_specs = [tok(w) for w in widths]
    return pl.pallas_call(
        functools.partial(_inproj_kernel, tm=tm, offs=offs),
        grid=(b, s // tm),
        in_specs=[
            tok(d),
            pl.BlockSpec((1, 6, d), lambda bi, j: (bi, 0, 0)),
            pl.BlockSpec((1, d), lambda bi, j: (0, 0)),
            pl.BlockSpec((d, n), lambda bi, j: (0, 0)),
            pl.BlockSpec((1, LANES), lambda bi, j: (0, 0)),
        ],
        out_specs=out_specs,
        out_shape=out_shape,
        scratch_shapes=[pltpu.VMEM((N_HEADS_FOX, LANES), F32)],
        compiler_params=pltpu.CompilerParams(
            dimension_semantics=("arbitrary", "arbitrary"),
            vmem_limit_bytes=VMEM_LIMIT),
        name="inproj",
    )(x, mod3, g_attn.reshape(1, d), w_ext, bf_pad)


def _swa_kernel(sink_ref, q_ref, kc_ref, kp_ref, vc_ref, vp_ref, o_ref, *, tq):
    min_key = jnp.where(pl.program_id(1) == 0, BLOCK, 0)
    lane = lax.broadcasted_iota(jnp.int32, (BLOCK, LANES), 1)
    lo = lane < HEAD_DIM
    qi = lax.broadcasted_iota(jnp.int32, (BLOCK, 2 * BLOCK), 0)
    kj = lax.broadcasted_iota(jnp.int32, (BLOCK, 2 * BLOCK), 1)
    dist_i = (BLOCK + qi) - kj
    dist = dist_i.astype(F32)
    band = (dist_i >= 0) & (dist_i < WINDOW)
    n_pairs = N_HEADS_SWA // 2
    group = N_HEADS_SWA // N_KV_SWA
    biases = [jnp.where(band, -(2.0 ** (-8.0 * (hd + 1) / N_HEADS_SWA)) * dist, NEG)
              for hd in range(N_HEADS_SWA)]
    has_key = kj >= min_key

    for r in range(tq // BLOCK):
        rows = slice(r * BLOCK, (r + 1) * BLOCK)
        if r == 0:
            k_prev, v_prev = kp_ref[0], vp_ref[0]
        else:
            prev = slice((r - 1) * BLOCK, r * BLOCK)
            k_prev, v_prev = kc_ref[0, prev, :], vc_ref[0, prev, :]
        k_all = jnp.concatenate([k_prev, kc_ref[0, rows, :]], axis=0)
        v_all = jnp.concatenate([v_prev, vc_ref[0, rows, :]], axis=0)
        for p in range(n_pairs):
            q = q_ref[0, rows, p * LANES:(p + 1) * LANES]
            kvh = (2 * p) // group
            k2 = k_all[:, kvh * LANES:(kvh + 1) * LANES]
            v2 = v_all[:, kvh * LANES:(kvh + 1) * LANES]
            outs = []
            for e in range(2):
                hd = 2 * p + e
                qm = jnp.where(lo if e == 0 else jnp.logical_not(lo), q, jnp.zeros_like(q))
                sc = lax.dot_general(qm, k2, _NT, preferred_element_type=F32) + biases[hd]
                if r == 0:
                    sc = jnp.where(has_key, sc, NEG)
                sink = sink_ref[hd]
                m = jnp.maximum(jnp.max(sc, axis=1, keepdims=True), sink)
                pr = jnp.exp(sc - m)
                den = jnp.sum(pr, axis=1, keepdims=True) + jnp.exp(sink - m)
                o = jnp.dot(pr.astype(BF16), v2, preferred_element_type=F32)
                outs.append(o / den)
            o_ref[0, rows, p * LANES:(p + 1) * LANES] = jnp.where(lo, outs[0], outs[1]).astype(BF16)


def _swa(qa, ka2, va2, sinks, tq):
    b, s, dq = qa.shape
    dk = ka2.shape[2]
    nsub = tq // BLOCK
    cur = lambda w: pl.BlockSpec((1, tq, w), lambda bi, i: (bi, i, 0))
    prev = lambda w: pl.BlockSpec((1, BLOCK, w), lambda bi, i: (bi, jnp.maximum(i * nsub - 1, 0), 0))
    return pl.pallas_call(
        functools.partial(_swa_kernel, tq=tq),
        grid=(b, s // tq),
        in_specs=[
            pl.BlockSpec(memory_space=pltpu.SMEM),
            cur(dq), cur(dk), prev(dk), cur(dk), prev(dk),
        ],
        out_specs=cur(dq),
        out_shape=jax.ShapeDtypeStruct((b, s, dq), BF16),
        compiler_params=pltpu.CompilerParams(
            dimension_semantics=("arbitrary", "arbitrary"),
            vmem_limit_bytes=VMEM_LIMIT),
        name="swa",
    )(sinks.astype(F32), qa, ka2, ka2, va2, va2)


def _fox_kernel(q_ref, k_ref, v_ref, o_ref, *, tq, tk):
    seq = q_ref.shape[1]
    heads = range(2)
    blk = [slice(e * LANES, (e + 1) * LANES) for e in heads]

    def attend(e, q, m, acc, start, width, first_row=None):
        ks = k_ref[0, start:start + width, blk[e]]
        vs = v_ref[0, start:start + width, blk[e]]
        s = lax.dot_general(q, ks, _NT, preferred_element_type=F32)
        if first_row is not None:
            row = lax.broadcasted_iota(jnp.int32, s.shape, 0) + first_row
            col = lax.broadcasted_iota(jnp.int32, s.shape, 1)
            s = jnp.where(col <= row, s, NEG)
        m_new = jnp.maximum(m, jnp.max(s, axis=1, keepdims=True))
        p = jnp.exp(s - m_new).astype(BF16)
        acc = jnp.exp(m - m_new) * acc + jnp.dot(p, vs, preferred_element_type=F32)
        return m_new, acc

    rq = tq // FOX_DIAG_ROW_BLOCKS
    lane = lax.broadcasted_iota(jnp.int32, (tq, LANES), 1)
    for i in range(seq // tq):
        q_rows = slice(i * tq, (i + 1) * tq)
        finals = []
        for e in heads:
            q = q_ref[0, q_rows, blk[e]]
            m = jnp.full((tq, 1), NEG, F32)
            acc = jnp.zeros((tq, LANES), F32)
            for start in range(0, i * tq, tk):
                m, acc = attend(e, q, m, acc, start, min(tk, i * tq - start))
            accs = []
            for r in range(FOX_DIAG_ROW_BLOCKS):
                rows = slice(r * rq, (r + 1) * rq)
                _, a = attend(e, q[rows], m[rows], acc[rows], i * tq, (r + 1) * rq,
                              first_row=r * rq)
                accs.append(a)
            finals.append(jnp.concatenate(accs, axis=0))
        acc_e, acc_o = finals
        o_ref[0, q_rows, :] = jnp.where(lane < HEAD_DIM,
                                        acc_e / acc_e[:, HEAD_DIM:HEAD_DIM + 1],
                                        acc_o / acc_o[:, 0:1]).astype(BF16)


def _fox(qf, kf, vf, tq, tk):
    b, s, width = qf.shape
    n_pairs = width // (2 * LANES)
    pair = pl.BlockSpec((1, s, 2 * LANES), lambda bi, hp: (bi, 0, hp))
    return pl.pallas_call(
        functools.partial(_fox_kernel, tq=tq, tk=tk),
        grid=(b, n_pairs),
        in_specs=[pair, pair, pair],
        out_specs=pl.BlockSpec((1, s, LANES), lambda bi, hp: (bi, 0, hp)),
        out_shape=jax.ShapeDtypeStruct((b, s, n_pairs * LANES), BF16),
        compiler_params=pltpu.CompilerParams(
            dimension_semantics=("arbitrary", "arbitrary"),
            vmem_limit_bytes=VMEM_LIMIT),
        name="fox",
    )(qf, kf, vf)


def _shift_rows(u, prev, k):
    r = pltpu.roll(u, k, axis=0)
    top_idx = lax.broadcasted_iota(jnp.int32, (SUBLANES, u.shape[1]), 0)
    top = jnp.where(top_idx < k, pltpu.roll(prev, k, axis=0), r[0:SUBLANES])
    return jnp.concatenate([top, r[SUBLANES:]], axis=0)


def _causal_conv(u, prev, w, bias):
    return (w[0:1] * _shift_rows(u, prev, 2) + w[1:2] * _shift_rows(u, prev, 1)
            + w[2:3] * u + bias)


def _mlp_kernel(x_ref, ya_ref, yb_ref, mod_ref, gm_ref, gf_ref, wo_ref, wu_ref, cw_ref, cb_ref,
                wd_ref, o_ref, prev_ref, h2_ref, x1_ref, *, tm, d_ff, tf):
    @pl.when(pl.program_id(1) == 0)
    def _():
        prev_ref[...] = jnp.zeros_like(prev_ref)

    ga1 = mod_ref[0, 2:3, :]
    sh2 = mod_ref[0, 3:4, :]
    sc2 = mod_ref[0, 4:5, :]
    ga2 = mod_ref[0, 5:6, :]
    q_a = ya_ref.shape[2]
    attn = (jnp.dot(ya_ref[0], wo_ref[0:q_a, :], preferred_element_type=F32)
            + jnp.dot(yb_ref[0], wo_ref[q_a:, :], preferred_element_type=F32))
    x1 = x_ref[0] + ga1 * attn
    h2_ref[...] = (_rms(x1, gm_ref[...]) * (1.0 + sc2) + sh2).astype(BF16)
    x1_ref[...] = x1

    def conv_cols(col0):
        cols = slice(col0, col0 + tf)
        u = jnp.dot(h2_ref[...], wu_ref[:, cols], preferred_element_type=F32)
        y = _causal_conv(u, prev_ref[:, cols], cw_ref[:, cols], cb_ref[:, cols])
        prev_ref[:, cols] = u[tm - SUBLANES:tm]
        return y

    acts = []
    for c in range(d_ff // tf):
        ca = conv_cols(c * tf)
        cg = conv_cols(d_ff + c * tf)
        acts.append((cg * jax.nn.sigmoid(cg) * ca).astype(BF16))
    mlp = None
    for c, act in enumerate(acts):
        part = jnp.dot(act, wd_ref[c * tf:(c + 1) * tf, :], preferred_element_type=F32)
        mlp = part if mlp is None else mlp + part
    o_ref[0] = _rms(x1_ref[...] + ga2 * mlp, gf_ref[...])


def _mlp(x, ya, yb, mod3, g_mlp, g_final, wo, wu, cw, cb, wd, tm, tf):
    b, s, d = x.shape
    d_ff = wd.shape[0]
    tok = lambda w: pl.BlockSpec((1, tm, w), lambda bi, j: (bi, j, 0))

    def const(shape):
        nd = len(shape)
        return pl.BlockSpec(shape, lambda bi, j: (0,) * nd, pipeline_mode=pl.Buffered(1))

    return pl.pallas_call(
        functools.partial(_mlp_kernel, tm=tm, d_ff=d_ff, tf=tf),
        grid=(b, s // tm),
        in_specs=[
            tok(d), tok(ya.shape[2]), tok(yb.shape[2]),
            pl.BlockSpec((1, 6, d), lambda bi, j: (bi, 0, 0)),
            const((1, d)), const((1, d)),
            const(wo.shape), const(wu.shape), const(cw.shape), const(cb.shape), const(wd.shape),
        ],
        out_specs=tok(d),
        out_shape=jax.ShapeDtypeStruct((b, s, d), F32),
        scratch_shapes=[
            pltpu.VMEM((SUBLANES, 2 * d_ff), F32),
            pltpu.VMEM((tm, d), BF16),
            pltpu.VMEM((tm, d), F32),
        ],
        compiler_params=pltpu.CompilerParams(
            dimension_semantics=("arbitrary", "arbitrary"),
            vmem_limit_bytes=VMEM_LIMIT),
        name="mlp",
    )(x, ya, yb, mod3, g_mlp.reshape(1, d), g_final.reshape(1, d), wo, wu, cw, cb, wd)


def _inproj_weights(w_in, b_f):
    d = w_in.shape[0]
    q_a = N_HEADS_SWA * HEAD_DIM
    kv_a = N_KV_SWA * HEAD_DIM
    q_b = N_HEADS_FOX * HEAD_DIM
    assert w_in.shape == (d, q_a + 2 * kv_a + 3 * q_b + N_HEADS_FOX)
    o = 0
    w_qa = w_in[:, o:o + q_a]; o += q_a
    w_ka = w_in[:, o:o + kv_a]; o += kv_a
    w_va = w_in[:, o:o + kv_a]; o += kv_a
    w_rest = w_in[:, o:o + 3 * q_b]; o += 3 * q_b
    w_f = w_in[:, o:]

    def dup(w):
        w = w.reshape(d, N_KV_SWA, 1, HEAD_DIM)
        return jnp.broadcast_to(w, (d, N_KV_SWA, 2, HEAD_DIM)).reshape(d, 2 * kv_a)

    parts = [("f", jnp.pad(w_f, ((0, 0), (0, LANES - N_HEADS_FOX)))),
             ("qa", w_qa * SCALE), ("ka", dup(w_ka)), ("va", dup(w_va)),
             ("qb", w_rest[:, :q_b] * SCALE), ("kb", w_rest[:, q_b:2 * q_b]),
             ("vb", w_rest[:, 2 * q_b:])]
    w_ext = jnp.concatenate([w for _, w in parts], axis=1).astype(BF16)
    offs, o = {}, 0
    for name, w in parts:
        offs[name] = (o, o + w.shape[1])
        o += w.shape[1]
    bf_pad = jnp.pad(b_f.astype(F32), (0, LANES - N_HEADS_FOX)).reshape(1, LANES)
    return w_ext, offs, bf_pad


def kernel(x, c, w_ada, b_ada, g_attn, w_in, b_f, sinks, w_out, g_mlp, w_up, conv_w, conv_b,
           w_down, g_final):
    b, s, d = x.shape
    d_ff = w_down.shape[0]
    assert w_up.shape == (d, 2 * d_ff) and w_out.shape == (d, d)

    mod3 = _ada(c, w_ada, b_ada).reshape(b, 6, d)
    w_ext, offs, bf_pad = _inproj_weights(w_in, b_f)
    qa, ka2, va2, qf, kf, vf = _inproj(x, mod3, g_attn, w_ext, bf_pad, offs, tm=512)
    ya = _swa(qa, ka2, va2, sinks, tq=512)
    yb = _fox(qf, kf, vf, tq=1024, tk=s)

    return _mlp(x, ya, yb, mod3, g_mlp, g_final, w_out.astype(BF16), w_up.astype(BF16),
                conv_w, conv_b.reshape(1, 2 * d_ff), w_down.astype(BF16), tm=512, tf=256)
```

```python
import functools

import jax
import jax.numpy as jnp
from jax import lax
from jax.experimental import pallas as pl
from jax.experimental.pallas import tpu as pltpu

F32 = jnp.float32
BF16 = jnp.bfloat16

HEAD_DIM = 64
N_HEADS_SWA = 8
N_KV_SWA = 2
N_HEADS_FOX = 8
WINDOW = 128
BLOCK = 128
CONV_WIDTH = 3
EPS = 1e-6
NEG = -1e30
SCALE = HEAD_DIM ** -0.5

LANES = 128
SUBLANES = 8
VMEM_LIMIT = 56 * 1024 * 1024
FOX_DIAG_ROW_BLOCKS = 2

_NT = (((1,), (1,)), ((), ()))


def _rms(x, g):
    return x * lax.rsqrt(jnp.mean(x * x, axis=-1, keepdims=True) + EPS) * g


def _ada_kernel(ct_ref, w_ref, b_ref, o_ref, sb_ref, *, nb, tn):
    @pl.when(pl.program_id(0) == 0)
    def _():
        ct = ct_ref[...]
        s = ct * jax.nn.sigmoid(ct)
        for b in range(nb):
            sb_ref[b] = jnp.broadcast_to(s[:, b:b + 1], sb_ref.shape[1:])

    for cchunk in range(tn // LANES):
        sl = slice(cchunk * LANES, (cchunk + 1) * LANES)
        w = w_ref[:, sl]
        for b in range(nb):
            o_ref[b:b + 1, sl] = jnp.sum(w * sb_ref[b], axis=0, keepdims=True) + b_ref[:, sl]


def _ada(c, w_ada, b_ada):
    nb, d = c.shape
    n = w_ada.shape[1]
    tn = 512
    return pl.pallas_call(
        functools.partial(_ada_kernel, nb=nb, tn=tn),
        grid=(n // tn,),
        in_specs=[
            pl.BlockSpec((d, nb), lambda j: (0, 0)),
            pl.BlockSpec((d, tn), lambda j: (0, j)),
            pl.BlockSpec((1, tn), lambda j: (0, j)),
        ],
        out_specs=pl.BlockSpec((nb, tn), lambda j: (0, j)),
        out_shape=jax.ShapeDtypeStruct((nb, n), F32),
        scratch_shapes=[pltpu.VMEM((nb, d, LANES), F32)],
        compiler_params=pltpu.CompilerParams(dimension_semantics=("arbitrary",)),
        name="ada",
    )(c.T, w_ada, b_ada.reshape(1, n))


def _prefix_rows(v):
    row = lax.broadcasted_iota(jnp.int32, v.shape, 0)
    d = 1
    while d < v.shape[0]:
        v = v + jnp.where(row >= d, pltpu.roll(v, d, axis=0), 0.0)
        d *= 2
    return v


def _inproj_kernel(x_ref, mod_ref, g_ref, w_ref, bf_ref,
                   qa_ref, ka_ref, va_ref, qf_ref, kf_ref, vf_ref,
                   carry_ref, *, tm, offs):
    x = x_ref[0]
    sh = mod_ref[0, 0:1, :]
    sc = mod_ref[0, 1:2, :]
    h = (_rms(x, g_ref[...]) * (1.0 + sc) + sh).astype(BF16)
    lane = lax.broadcasted_iota(jnp.int32, (tm, LANES), 1)
    low = lane < HEAD_DIM

    def proj(first, last):
        return jnp.dot(h, w_ref[:, offs[first][0]:offs[last][1]], preferred_element_type=F32)

    za = proj("f", "va")
    part = lambda name: za[:, offs[name][0] - offs["f"][0]:offs[name][1] - offs["f"][0]]
    qa_ref[0] = part("qa").astype(BF16)
    for name, ref in (("ka", ka_ref), ("va", va_ref)):
        kv = part(name)
        swapped = pltpu.roll(kv, HEAD_DIM, axis=1)
        ref[0, :, 0:LANES] = jnp.where(low, kv, swapped).astype(BF16)
        ref[0, :, LANES:2 * LANES] = jnp.where(low, swapped, kv).astype(BF16)

    f = part("f") + bf_ref[...]
    lf = -(jnp.maximum(-f, 0.0) + jnp.log1p(jnp.exp(-jnp.abs(f))))

    @pl.when(pl.program_id(1) == 0)
    def _():
        carry_ref[...] = jnp.zeros_like(carry_ref)

    fcum = _prefix_rows(lf) + carry_ref[0:1, :]
    carry_ref[...] = jnp.broadcast_to(fcum[tm - 1:tm, :], carry_ref.shape)

    def spare_lanes(hd):
        odd = hd % 2
        return (jnp.logical_not(low) if odd else low), (0 if odd else HEAD_DIM)

    def k_spare(hd):
        _, a0 = spare_lanes(hd)
        nf = -jnp.broadcast_to(fcum[:, hd:hd + 1], (tm, LANES))
        hi = nf.astype(BF16).astype(F32)
        rem = nf - hi
        mid = rem.astype(BF16).astype(F32)
        return jnp.where(lane == a0, hi,
                         jnp.where(lane == a0 + 1, mid,
                                   jnp.where(lane == a0 + 2, rem - mid, 0.0)))

    def q_spare(hd):
        _, a0 = spare_lanes(hd)
        return jnp.where((lane >= a0) & (lane < a0 + 3), 1.0, 0.0)

    def v_spare(hd):
        _, a0 = spare_lanes(hd)
        return jnp.where(lane == a0, 1.0, 0.0)

    for name, ref, spare in (("qb", qf_ref, q_spare), ("kb", kf_ref, k_spare),
                             ("vb", vf_ref, v_spare)):
        z = proj(name, name)
        for hd in range(N_HEADS_FOX):
            pair = hd // 2
            data, _ = spare_lanes(hd)
            ref[0, :, hd * LANES:(hd + 1) * LANES] = jnp.where(
                data, z[:, pair * LANES:(pair + 1) * LANES], spare(hd)).astype(BF16)


def _inproj(x, mod3, g_attn, w_ext, bf_pad, offs, tm):
    b, s, d = x.shape
    n = w_ext.shape[1]

    def tok(width):
        return pl.BlockSpec((1, tm, width), lambda bi, j: (bi, j, 0))

    fox_width = N_HEADS_FOX * LANES
    kv_width = N_KV_SWA * LANES
    widths = (offs["qa"][1] - offs["qa"][0], kv_width, kv_width) + (fox_width,) * 3
    out_shape = [jax.ShapeDtypeStruct((b, s, w), BF16) for w in widths]
    out_specs = [tok(w) for w in widths]
    return pl.pallas_call(
        functools.partial(_inproj_kernel, tm=tm, offs=offs),
        grid=(b, s // tm),
        in_specs=[
            tok(d),
            pl.BlockSpec((1, 6, d), lambda bi, j: (bi, 0, 0)),
            pl.BlockSpec((1, d), lambda bi, j: (0, 0)),
            pl.BlockSpec((d, n), lambda bi, j: (0, 0)),
            pl.BlockSpec((1, LANES), lambda bi, j: (0, 0)),
        ],
        out_specs=out_specs,
        out_shape=out_shape,
        scratch_shapes=[pltpu.VMEM((N_HEADS_FOX, LANES), F32)],
        compiler_params=pltpu.CompilerParams(
            dimension_semantics=("arbitrary", "arbitrary"),
            vmem_limit_bytes=VMEM_LIMIT),
        name="inproj",
    )(x, mod3, g_attn.reshape(1, d), w_ext, bf_pad)


def _attn_kernel(sink_ref, slope_ref, qa_ref, ka_ref, va_ref, qf_ref, kf_ref, vf_ref,
                 oa_ref, of_ref, *, tq, tk):
    seq = qf_ref.shape[1]
    hp = pl.program_id(1)
    heads = range(2)
    blk = [slice(e * LANES, (e + 1) * LANES) for e in heads]

    def attend(e, q, m, acc, start, width, first_row=None):
        ks = kf_ref[0, start:start + width, blk[e]]
        vs = vf_ref[0, start:start + width, blk[e]]
        s = lax.dot_general(q, ks, _NT, preferred_element_type=F32)
        if first_row is not None:
            row = lax.broadcasted_iota(jnp.int32, s.shape, 0) + first_row
            col = lax.broadcasted_iota(jnp.int32, s.shape, 1)
            s = jnp.where(col <= row, s, NEG)
        m_new = jnp.maximum(m, jnp.max(s, axis=1, keepdims=True))
        p = jnp.exp(s - m_new).astype(BF16)
        acc = jnp.exp(m - m_new) * acc + jnp.dot(p, vs, preferred_element_type=F32)
        return m_new, acc

    def fox_tile(i):
        rq = tq // FOX_DIAG_ROW_BLOCKS
        q_rows = slice(i * tq, (i + 1) * tq)
        finals = []
        for e in heads:
            q = qf_ref[0, q_rows, blk[e]]
            m = jnp.full((tq, 1), NEG, F32)
            acc = jnp.zeros((tq, LANES), F32)
            for start in range(0, i * tq, tk):
                m, acc = attend(e, q, m, acc, start, min(tk, i * tq - start))
            accs = []
            for r in range(FOX_DIAG_ROW_BLOCKS):
                rows = slice(r * rq, (r + 1) * rq)
                _, a = attend(e, q[rows], m[rows], acc[rows], i * tq, (r + 1) * rq,
                              first_row=r * rq)
                accs.append(a)
            finals.append(jnp.concatenate(accs, axis=0))
        acc_e, acc_o = finals
        lane = lax.broadcasted_iota(jnp.int32, (tq, LANES), 1)
        of_ref[0, q_rows, :] = jnp.where(lane < HEAD_DIM,
                                         acc_e / acc_e[:, HEAD_DIM:HEAD_DIM + 1],
                                         acc_o / acc_o[:, 0:1]).astype(BF16)

    low = lax.broadcasted_iota(jnp.int32, (BLOCK, LANES), 1) < HEAD_DIM
    qi = lax.broadcasted_iota(jnp.int32, (BLOCK, 2 * BLOCK), 0)
    kj = lax.broadcasted_iota(jnp.int32, (BLOCK, 2 * BLOCK), 1)
    dist_i = (BLOCK + qi) - kj
    band = (dist_i >= 0) & (dist_i < WINDOW)
    swa_bias = [jnp.where(band, -slope_ref[2 * hp + e] * dist_i.astype(F32), NEG) for e in heads]
    swa_sink = [sink_ref[2 * hp + e] for e in heads]

    def swa_block(r):
        rows = slice(r * BLOCK, (r + 1) * BLOCK)
        keys = slice(max(r - 1, 0) * BLOCK, (r + 1) * BLOCK)
        q = qa_ref[0, rows, :]
        k2 = ka_ref[0, keys, :]
        v2 = va_ref[0, keys, :]
        outs = []
        for e in heads:
            qm = jnp.where(low if e == 0 else jnp.logical_not(low), q, jnp.zeros_like(q))
            bias = swa_bias[e] if r else swa_bias[e][:, BLOCK:]
            sc = lax.dot_general(qm, k2, _NT, preferred_element_type=F32) + bias
            m = jnp.maximum(jnp.max(sc, axis=1, keepdims=True), swa_sink[e])
            pr = jnp.exp(sc - m)
            den = jnp.sum(pr, axis=1, keepdims=True) + jnp.exp(swa_sink[e] - m)
            outs.append(jnp.dot(pr.astype(BF16), v2, preferred_element_type=F32) / den)
        oa_ref[0, rows, :] = jnp.where(low, outs[0], outs[1]).astype(BF16)

    for i in range(seq // tq):
        fox_tile(i)
        for r in range(i * tq // BLOCK, (i + 1) * tq // BLOCK):
            swa_block(r)


def _attn(qa, ka2, va2, qf, kf, vf, sinks, tq, tk):
    b, s, _ = qa.shape
    n_pairs = qa.shape[2] // LANES
    group_pairs = n_pairs // (ka2.shape[2] // LANES)
    slopes = jnp.exp2(-8.0 * (jnp.arange(N_HEADS_SWA, dtype=F32) + 1) / N_HEADS_SWA)
    smem = pl.BlockSpec(memory_space=pltpu.SMEM)
    one = pl.BlockSpec((1, s, LANES), lambda bi, hp: (bi, 0, hp))
    kv = pl.BlockSpec((1, s, LANES), lambda bi, hp: (bi, 0, hp // group_pairs))
    two = pl.BlockSpec((1, s, 2 * LANES), lambda bi, hp: (bi, 0, hp))
    out = jax.ShapeDtypeStruct((b, s, n_pairs * LANES), BF16)
    return pl.pallas_call(
        functools.partial(_attn_kernel, tq=tq, tk=tk),
        grid=(b, n_pairs),
        in_specs=[smem, smem, one, kv, kv, two, two, two],
        out_specs=[one, one],
        out_shape=[out, out],
        compiler_params=pltpu.CompilerParams(
            dimension_semantics=("arbitrary", "arbitrary"),
            vmem_limit_bytes=VMEM_LIMIT),
        name="attn",
    )(sinks.astype(F32), slopes, qa, ka2, va2, qf, kf, vf)


def _shift_rows(u, prev, k):
    r = pltpu.roll(u, k, axis=0)
    top_idx = lax.broadcasted_iota(jnp.int32, (SUBLANES, u.shape[1]), 0)
    top = jnp.where(top_idx < k, pltpu.roll(prev, k, axis=0), r[0:SUBLANES])
    return jnp.concatenate([top, r[SUBLANES:]], axis=0)


def _causal_conv(u, prev, w, bias):
    return (w[0:1] * _shift_rows(u, prev, 2) + w[1:2] * _shift_rows(u, prev, 1)
            + w[2:3] * u + bias)


def _mlp_kernel(x_ref, ya_ref, yb_ref, mod_ref, gm_ref, gf_ref, wo_ref, wu_ref, cw_ref, cb_ref,
                wd_ref, o_ref, prev_ref, h2_ref, x1_ref, *, tm, d_ff, tf):
    @pl.when(pl.program_id(1) == 0)
    def _():
        prev_ref[...] = jnp.zeros_like(prev_ref)

    ga1 = mod_ref[0, 2:3, :]
    sh2 = mod_ref[0, 3:4, :]
    sc2 = mod_ref[0, 4:5, :]
    ga2 = mod_ref[0, 5:6, :]
    q_a = ya_ref.shape[2]
    attn = (jnp.dot(ya_ref[0], wo_ref[0:q_a, :], preferred_element_type=F32)
            + jnp.dot(yb_ref[0], wo_ref[q_a:, :], preferred_element_type=F32))
    x1 = x_ref[0] + ga1 * attn
    h2_ref[...] = (_rms(x1, gm_ref[...]) * (1.0 + sc2) + sh2).astype(BF16)
    x1_ref[...] = x1

    def conv_cols(col0):
        cols = slice(col0, col0 + tf)
        u = jnp.dot(h2_ref[...], wu_ref[:, cols], preferred_element_type=F32)
        y = _causal_conv(u, prev_ref[:, cols], cw_ref[:, cols], cb_ref[:, cols])
        prev_ref[:, cols] = u[tm - SUBLANES:tm]
        return y

    acts = []
    for c in range(d_ff // tf):
        ca = conv_cols(c * tf)
        cg = conv_cols(d_ff + c * tf)
        acts.append((cg * jax.nn.sigmoid(cg) * ca).astype(BF16))
    mlp = None
    for c, act in enumerate(acts):
        part = jnp.dot(act, wd_ref[c * tf:(c + 1) * tf, :], preferred_element_type=F32)
        mlp = part if mlp is None else mlp + part
    o_ref[0] = _rms(x1_ref[...] + ga2 * mlp, gf_ref[...])


def _mlp(x, ya, yb, mod3, g_mlp, g_final, wo, wu, cw, cb, wd, tm, tf):
    b, s, d = x.shape
    d_ff = wd.shape[0]
    tok = lambda w: pl.BlockSpec((1, tm, w), lambda bi, j: (bi, j, 0))

    def const(shape):
        nd = len(shape)
        return pl.BlockSpec(shape, lambda bi, j: (0,) * nd, pipeline_mode=pl.Buffered(1))

    return pl.pallas_call(
        functools.partial(_mlp_kernel, tm=tm, d_ff=d_ff, tf=tf),
        grid=(b, s // tm),
        in_specs=[
            tok(d), tok(ya.shape[2]), tok(yb.shape[2]),
            pl.BlockSpec((1, 6, d), lambda bi, j: (bi, 0, 0)),
            const((1, d)), const((1, d)),
            const(wo.shape), const(wu.shape), const(cw.shape), const(cb.shape), const(wd.shape),
        ],
        out_specs=tok(d),
        out_shape=jax.ShapeDtypeStruct((b, s, d), F32),
        scratch_shapes=[
            pltpu.VMEM((SUBLANES, 2 * d_ff), F32),
            pltpu.VMEM((tm, d), BF16),
            pltpu.VMEM((tm, d), F32),
        ],
        compiler_params=pltpu.CompilerParams(
            dimension_semantics=("arbitrary", "arbitrary"),
            vmem_limit_bytes=VMEM_LIMIT),
        name="mlp",
    )(x, ya, yb, mod3, g_mlp.reshape(1, d), g_final.reshape(1, d), wo, wu, cw, cb, wd)


def _inproj_weights(w_in, b_f):
    d = w_in.shape[0]
    q_a = N_HEADS_SWA * HEAD_DIM
    kv_a = N_KV_SWA * HEAD_DIM
    q_b = N_HEADS_FOX * HEAD_DIM
    assert w_in.shape == (d, q_a + 2 * kv_a + 3 * q_b + N_HEADS_FOX)
    assert kv_a == LANES
    o = 0
    w_qa = w_in[:, o:o + q_a]; o += q_a
    w_ka = w_in[:, o:o + kv_a]; o += kv_a
    w_va = w_in[:, o:o + kv_a]; o += kv_a
    w_rest = w_in[:, o:o + 3 * q_b]; o += 3 * q_b
    w_f = w_in[:, o:]

    parts = [("f", jnp.pad(w_f, ((0, 0), (0, LANES - N_HEADS_FOX)))),
             ("qa", w_qa * SCALE), ("ka", w_ka), ("va", w_va),
             ("qb", w_rest[:, :q_b] * SCALE), ("kb", w_rest[:, q_b:2 * q_b]),
             ("vb", w_rest[:, 2 * q_b:])]
    w_ext = jnp.concatenate([w for _, w in parts], axis=1).astype(BF16)
    offs, o = {}, 0
    for name, w in parts:
        offs[name] = (o, o + w.shape[1])
        o += w.shape[1]
    bf_pad = jnp.pad(b_f.astype(F32), (0, LANES - N_HEADS_FOX)).reshape(1, LANES)
    return w_ext, offs, bf_pad


def kernel(x, c, w_ada, b_ada, g_attn, w_in, b_f, sinks, w_out, g_mlp, w_up, conv_w, conv_b,
           w_down, g_final):
    b, s, d = x.shape
    d_ff = w_down.shape[0]
    assert w_up.shape == (d, 2 * d_ff) and w_out.shape == (d, d)

    mod3 = _ada(c, w_ada, b_ada).reshape(b, 6, d)
    w_ext, offs, bf_pad = _inproj_weights(w_in, b_f)
    qa, ka2, va2, qf, kf, vf = _inproj(x, mod3, g_attn, w_ext, bf_pad, offs, tm=512)
    ya, yb = _attn(qa, ka2, va2, qf, kf, vf, sinks, tq=1024, tk=s)
    return _mlp(x, ya, yb, mod3, g_mlp, g_final, w_out.astype(BF16), w_up.astype(BF16),
                conv_w, conv_b.reshape(1, 2 * d_ff), w_down.astype(BF16), tm=512, tf=256)
```

```python
import functools

import jax
import jax.numpy as jnp
from jax import lax
from jax.experimental import pallas as pl
from jax.experimental.pallas import tpu as pltpu

F32 = jnp.float32
BF16 = jnp.bfloat16

HEAD_DIM = 64
N_HEADS_SWA = 8
N_KV_SWA = 2
N_HEADS_FOX = 8
WINDOW = 128
BLOCK = 128
CONV_WIDTH = 3
EPS = 1e-6
NEG = -1e30
SCALE = HEAD_DIM ** -0.5
LOG2E = 1.4426950408889634
Q_SCALE = SCALE * LOG2E

LANES = 128
SUBLANES = 8
VMEM_LIMIT = 56 * 1024 * 1024
FOX_DIAG_ROW_BLOCKS = 2

_NT = (((1,), (1,)), ((), ()))


def _rms(x, g):
    return x * lax.rsqrt(jnp.mean(x * x, axis=-1, keepdims=True) + EPS) * g


ADA_ROWS = 16


def _ada_kernel(c_ref, w_ref, b_ref, o_ref, lhs_ref, *, nb):
    @pl.when(pl.program_id(0) == 0)
    def _():
        c = c_ref[...]
        s = c * jax.nn.sigmoid(c)
        hi = s.astype(BF16).astype(F32)
        rem = s - hi
        mid = rem.astype(BF16).astype(F32)
        lhs_ref[...] = jnp.zeros_like(lhs_ref)
        lhs_ref[0:nb, :] = hi
        lhs_ref[nb:2 * nb, :] = mid
        lhs_ref[2 * nb:3 * nb, :] = rem - mid

    lhs = lhs_ref[...].astype(BF16)
    w = w_ref[...]
    w_hi = w.astype(BF16)
    w_lo = (w - w_hi.astype(F32)).astype(BF16)
    acc = (jnp.dot(lhs, w_hi, preferred_element_type=F32)
           + jnp.dot(lhs, w_lo, preferred_element_type=F32))
    o_ref[...] = acc[0:nb] + acc[nb:2 * nb] + acc[2 * nb:3 * nb] + b_ref[...]


def _ada(c, w_ada, b_ada):
    nb, d = c.shape
    n = w_ada.shape[1]
    tn = 1024
    assert 3 * nb <= ADA_ROWS
    return pl.pallas_call(
        functools.partial(_ada_kernel, nb=nb),
        grid=(n // tn,),
        in_specs=[
            pl.BlockSpec((nb, d), lambda j: (0, 0)),
            pl.BlockSpec((d, tn), lambda j: (0, j)),
            pl.BlockSpec((1, tn), lambda j: (0, j)),
        ],
        out_specs=pl.BlockSpec((nb, tn), lambda j: (0, j)),
        out_shape=jax.ShapeDtypeStruct((nb, n), F32),
        scratch_shapes=[pltpu.VMEM((ADA_ROWS, d), F32)],
        compiler_params=pltpu.CompilerParams(dimension_semantics=("arbitrary",)),
        name="ada",
    )(c, w_ada, b_ada.reshape(1, n))


def _prefix_rows(v):
    row = lax.broadcasted_iota(jnp.int32, v.shape, 0)
    d = 1
    while d < v.shape[0]:
        v = v + jnp.where(row >= d, pltpu.roll(v, d, axis=0), 0.0)
        d *= 2
    return v


def _inproj_kernel(x_ref, mod_ref, g_ref, w_ref, bf_ref,
                   qa_ref, ka_ref, va_ref, qf_ref, kf_ref, vf_ref,
                   carry_ref, *, tm, offs):
    x = x_ref[0]
    sh = mod_ref[0, 0:1, :]
    sc = mod_ref[0, 1:2, :]
    h = (_rms(x, g_ref[...]) * (1.0 + sc) + sh).astype(BF16)
    lane = lax.broadcasted_iota(jnp.int32, (tm, LANES), 1)
    low = lane < HEAD_DIM

    def proj(first, last):
        return jnp.dot(h, w_ref[:, offs[first][0]:offs[last][1]], preferred_element_type=F32)

    f = proj("f", "f") + bf_ref[...]
    za = proj("qa", "va")
    part = lambda name: za[:, offs[name][0] - offs["qa"][0]:offs[name][1] - offs["qa"][0]]
    qa_ref[0] = part("qa").astype(BF16)
    for name, ref in (("ka", ka_ref), ("va", va_ref)):
        kv = part(name)
        swapped = pltpu.roll(kv, HEAD_DIM, axis=1)
        ref[0, :, 0:LANES] = jnp.where(low, kv, swapped).astype(BF16)
        ref[0, :, LANES:2 * LANES] = jnp.where(low, swapped, kv).astype(BF16)

    lf = -(jnp.maximum(-f, 0.0) + jnp.log1p(jnp.exp(-jnp.abs(f))))

    @pl.when(pl.program_id(1) == 0)
    def _():
        carry_ref[...] = jnp.zeros_like(carry_ref)

    fcum = _prefix_rows(lf) + carry_ref[0:1, :]
    carry_ref[...] = jnp.broadcast_to(fcum[tm - 1:tm, :], carry_ref.shape)

    def spare_lanes(hd):
        odd = hd % 2
        return (jnp.logical_not(low) if odd else low), (0 if odd else HEAD_DIM)

    def k_spare(hd):
        _, a0 = spare_lanes(hd)
        nf = -LOG2E * jnp.broadcast_to(fcum[:, hd:hd + 1], (tm, LANES))
        hi = nf.astype(BF16).astype(F32)
        rem = nf - hi
        mid = rem.astype(BF16).astype(F32)
        return jnp.where(lane == a0, hi,
                         jnp.where(lane == a0 + 1, mid,
                                   jnp.where(lane == a0 + 2, rem - mid, 0.0)))

    def q_spare(hd):
        _, a0 = spare_lanes(hd)
        return jnp.where((lane >= a0) & (lane < a0 + 3), 1.0, 0.0)

    def v_spare(hd):
        _, a0 = spare_lanes(hd)
        return jnp.where(lane == a0, 1.0, 0.0)

    for name, ref, spare in (("qb", qf_ref, q_spare), ("kb", kf_ref, k_spare),
                             ("vb", vf_ref, v_spare)):
        z = proj(name, name)
        for hd in range(N_HEADS_FOX):
            pair = hd // 2
            data, _ = spare_lanes(hd)
            ref[0, :, hd * LANES:(hd + 1) * LANES] = jnp.where(
                data, z[:, pair * LANES:(pair + 1) * LANES], spare(hd)).astype(BF16)


def _inproj(x, mod3, g_attn, w_ext, bf_pad, offs, tm):
    b, s, d = x.shape
    n = w_ext.shape[1]

    def tok(width):
        return pl.BlockSpec((1, tm, width), lambda bi, j: (bi, j, 0))

    fox_width = N_HEADS_FOX * LANES
    kv_width = N_KV_SWA * LANES
    widths = (offs["qa"][1] - offs["qa"][0], kv_width, kv_width) + (fox_width,) * 3
    out_shape = [jax.ShapeDtypeStruct((b, s, w), BF16) for w in widths]
    out_specs = [tok(w) for w in widths]
    return pl.pallas_call(
        functools.partial(_inproj_kernel, tm=tm, offs=offs),
        grid=(b, s // tm),
        in_specs=[
            tok(d),
            pl.BlockSpec((1, 6, d), lambda bi, j: (bi, 0, 0)),
            pl.BlockSpec((1, d), lambda bi, j: (0, 0)),
            pl.BlockSpec((d, n), lambda bi, j: (0, 0)),
            pl.BlockSpec((1, LANES), lambda bi, j: (0, 0)),
        ],
        out_specs=out_specs,
        out_shape=out_shape,
        scratch_shapes=[pltpu.VMEM((N_HEADS_FOX, LANES), F32)],
        compiler_params=pltpu.CompilerParams(
            dimension_semantics=("arbitrary", "arbitrary"),
            vmem_limit_bytes=VMEM_LIMIT),
        name="inproj",
    )(x, mod3, g_attn.reshape(1, d), w_ext, bf_pad)


def _attn_kernel(sink_ref, slope_ref, qa_ref, ka_ref, va_ref, qf_ref, kf_ref, vf_ref,
                 oa_ref, of_ref, *, tq, tk):
    seq = qf_ref.shape[1]
    hp = pl.program_id(1)
    heads = range(2)
    blk = [slice(e * LANES, (e + 1) * LANES) for e in heads]

    def attend(e, q, m, acc, start, width, first_row=None):
        ks = kf_ref[0, start:start + width, blk[e]]
        vs = vf_ref[0, start:start + width, blk[e]]
        s = lax.dot_general(q, ks, _NT, preferred_element_type=F32)
        if first_row is not None:
            n_free = (first_row // LANES) * LANES
            tail = s[:, n_free:]
            row = lax.broadcasted_iota(jnp.int32, tail.shape, 0) + (first_row - n_free)
            col = lax.broadcasted_iota(jnp.int32, tail.shape, 1)
            tail = jnp.where(col <= row, tail, NEG)
            s = jnp.concatenate([s[:, :n_free], tail], axis=1) if n_free else tail
        m_new = jnp.maximum(m, jnp.max(s, axis=1, keepdims=True))
        p = jnp.exp2(s - m_new).astype(BF16)
        acc = jnp.exp2(m - m_new) * acc + jnp.dot(p, vs, preferred_element_type=F32)
        return m_new, acc

    def fox_tile(i):
        rq = tq // FOX_DIAG_ROW_BLOCKS
        q_rows = slice(i * tq, (i + 1) * tq)
        finals = []
        for e in heads:
            q = qf_ref[0, q_rows, blk[e]]
            m = jnp.full((tq, 1), NEG, F32)
            acc = jnp.zeros((tq, LANES), F32)
            for start in range(0, i * tq, tk):
                m, acc = attend(e, q, m, acc, start, min(tk, i * tq - start))
            accs = []
            for r in range(FOX_DIAG_ROW_BLOCKS):
                rows = slice(r * rq, (r + 1) * rq)
                _, a = attend(e, q[rows], m[rows], acc[rows], i * tq, (r + 1) * rq,
                              first_row=r * rq)
                accs.append(a)
            finals.append(jnp.concatenate(accs, axis=0))
        acc_e, acc_o = finals
        lane = lax.broadcasted_iota(jnp.int32, (tq, LANES), 1)
        of_ref[0, q_rows, :] = jnp.where(lane < HEAD_DIM,
                                         acc_e / acc_e[:, HEAD_DIM:HEAD_DIM + 1],
                                         acc_o / acc_o[:, 0:1]).astype(BF16)

    low = lax.broadcasted_iota(jnp.int32, (BLOCK, LANES), 1) < HEAD_DIM
    qi = lax.broadcasted_iota(jnp.int32, (BLOCK, 2 * BLOCK), 0)
    kj = lax.broadcasted_iota(jnp.int32, (BLOCK, 2 * BLOCK), 1)
    dist_i = (BLOCK + qi) - kj
    band = (dist_i >= 0) & (dist_i < WINDOW)
    swa_bias = [jnp.where(band, -(LOG2E * slope_ref[2 * hp + e]) * dist_i.astype(F32), NEG)
                for e in heads]
    swa_sink = [LOG2E * sink_ref[2 * hp + e] for e in heads]

    def swa_block(r):
        rows = slice(r * BLOCK, (r + 1) * BLOCK)
        keys = slice(max(r - 1, 0) * BLOCK, (r + 1) * BLOCK)
        q = qa_ref[0, rows, :]
        k2 = ka_ref[0, keys, :]
        v2 = va_ref[0, keys, :]
        outs = []
        for e in heads:
            qm = jnp.where(low if e == 0 else jnp.logical_not(low), q, jnp.zeros_like(q))
            bias = swa_bias[e] if r else swa_bias[e][:, BLOCK:]
            sc = lax.dot_general(qm, k2, _NT, preferred_element_type=F32) + bias
            m = jnp.maximum(jnp.max(sc, axis=1, keepdims=True), swa_sink[e])
            pr = jnp.exp2(sc - m)
            den = jnp.sum(pr, axis=1, keepdims=True) + jnp.exp2(swa_sink[e] - m)
            outs.append(jnp.dot(pr.astype(BF16), v2, preferred_element_type=F32) / den)
        oa_ref[0, rows, :] = jnp.where(low, outs[0], outs[1]).astype(BF16)

    for i in range(seq // tq):
        fox_tile(i)
        for r in range(i * tq // BLOCK, (i + 1) * tq // BLOCK):
            swa_block(r)


def _attn(qa, ka2, va2, qf, kf, vf, sinks, tq, tk):
    b, s, _ = qa.shape
    n_pairs = qa.shape[2] // LANES
    group_pairs = n_pairs // (ka2.shape[2] // LANES)
    slopes = jnp.exp2(-8.0 * (jnp.arange(N_HEADS_SWA, dtype=F32) + 1) / N_HEADS_SWA)
    smem = pl.BlockSpec(memory_space=pltpu.SMEM)
    one = pl.BlockSpec((1, s, LANES), lambda bi, hp: (bi, 0, hp))
    kv = pl.BlockSpec((1, s, LANES), lambda bi, hp: (bi, 0, hp // group_pairs))
    two = pl.BlockSpec((1, s, 2 * LANES), lambda bi, hp: (bi, 0, hp))
    out = jax.ShapeDtypeStruct((b, s, n_pairs * LANES), BF16)
    return pl.pallas_call(
        functools.partial(_attn_kernel, tq=tq, tk=tk),
        grid=(b, n_pairs),
        in_specs=[smem, smem, one, kv, kv, two, two, two],
        out_specs=[one, one],
        out_shape=[out, out],
        compiler_params=pltpu.CompilerParams(
            dimension_semantics=("arbitrary", "arbitrary"),
            vmem_limit_bytes=VMEM_LIMIT),
        name="attn",
    )(sinks.astype(F32), slopes, qa, ka2, va2, qf, kf, vf)


def _shift_rows(u, prev, k):
    r = pltpu.roll(u, k, axis=0)
    top_idx = lax.broadcasted_iota(jnp.int32, (SUBLANES, u.shape[1]), 0)
    top = jnp.where(top_idx < k, pltpu.roll(prev, k, axis=0), r[0:SUBLANES])
    return jnp.concatenate([top, r[SUBLANES:]], axis=0)


def _causal_conv(u, prev, w, bias):
    return (w[0:1] * _shift_rows(u, prev, 2) + w[1:2] * _shift_rows(u, prev, 1)
            + w[2:3] * u + bias)


def _mlp_kernel(x_ref, ya_ref, yb_ref, mod_ref, gm_ref, gf_ref, wo_ref, wu_ref, cw_ref, cb_ref,
                wd_ref, o_ref, prev_ref, h2_ref, x1_ref, *, tm, d_ff, tf):
    @pl.when(pl.program_id(1) == 0)
    def _():
        prev_ref[...] = jnp.zeros_like(prev_ref)

    ga1 = mod_ref[0, 2:3, :]
    sh2 = mod_ref[0, 3:4, :]
    sc2 = mod_ref[0, 4:5, :]
    ga2 = mod_ref[0, 5:6, :]
    q_a = ya_ref.shape[2]
    attn = (jnp.dot(ya_ref[0], wo_ref[0:q_a, :], preferred_element_type=F32)
            + jnp.dot(yb_ref[0], wo_ref[q_a:, :], preferred_element_type=F32))
    x1 = x_ref[0] + ga1 * attn
    h2_ref[...] = (_rms(x1, gm_ref[...]) * (1.0 + sc2) + sh2).astype(BF16)
    x1_ref[...] = x1

    def conv_cols(col0):
        cols = slice(col0, col0 + tf)
        u = jnp.dot(h2_ref[...], wu_ref[:, cols], preferred_element_type=F32)
        y = _causal_conv(u, prev_ref[:, cols], cw_ref[:, cols], cb_ref[:, cols])
        prev_ref[:, cols] = u[tm - SUBLANES:tm]
        return y

    acts = []
    for c in range(d_ff // tf):
        ca = conv_cols(c * tf)
        cg = conv_cols(d_ff + c * tf)
        acts.append((cg * jax.nn.sigmoid(cg) * ca).astype(BF16))
    mlp = None
    for c, act in enumerate(acts):
        part = jnp.dot(act, wd_ref[c * tf:(c + 1) * tf, :], preferred_element_type=F32)
        mlp = part if mlp is None else mlp + part
    o_ref[0] = _rms(x1_ref[...] + ga2 * mlp, gf_ref[...])


def _mlp(x, ya, yb, mod3, g_mlp, g_final, wo, wu, cw, cb, wd, tm, tf):
    b, s, d = x.shape
    d_ff = wd.shape[0]
    tok = lambda w: pl.BlockSpec((1, tm, w), lambda bi, j: (bi, j, 0))

    def const(shape):
        nd = len(shape)
        return pl.BlockSpec(shape, lambda bi, j: (0,) * nd, pipeline_mode=pl.Buffered(1))

    return pl.pallas_call(
        functools.partial(_mlp_kernel, tm=tm, d_ff=d_ff, tf=tf),
        grid=(b, s // tm),
        in_specs=[
            tok(d), tok(ya.shape[2]), tok(yb.shape[2]),
            pl.BlockSpec((1, 6, d), lambda bi, j: (bi, 0, 0)),
            const((1, d)), const((1, d)),
            const(wo.shape), const(wu.shape), const(cw.shape), const(cb.shape), const(wd.shape),
        ],
        out_specs=tok(d),
        out_shape=jax.ShapeDtypeStruct((b, s, d), F32),
        scratch_shapes=[
            pltpu.VMEM((SUBLANES, 2 * d_ff), F32),
            pltpu.VMEM((tm, d), BF16),
            pltpu.VMEM((tm, d), F32),
        ],
        compiler_params=pltpu.CompilerParams(
            dimension_semantics=("arbitrary", "arbitrary"),
            vmem_limit_bytes=VMEM_LIMIT),
        name="mlp",
    )(x, ya, yb, mod3, g_mlp.reshape(1, d), g_final.reshape(1, d), wo, wu, cw, cb, wd)


def _inproj_weights(w_in, b_f):
    d = w_in.shape[0]
    q_a = N_HEADS_SWA * HEAD_DIM
    kv_a = N_KV_SWA * HEAD_DIM
    q_b = N_HEADS_FOX * HEAD_DIM
    assert w_in.shape == (d, q_a + 2 * kv_a + 3 * q_b + N_HEADS_FOX)
    assert kv_a == LANES
    offs, o = {}, 0
    for name, width in (("qa", q_a), ("ka", kv_a), ("va", kv_a), ("qb", q_b), ("kb", q_b),
                        ("vb", q_b), ("f", LANES)):
        offs[name] = (o, o + width)
        o += width
    col = jnp.arange(o)
    is_q = (col < offs["qa"][1]) | ((col >= offs["qb"][0]) & (col < offs["qb"][1]))
    col_scale = jnp.where(is_q, Q_SCALE, 1.0).astype(F32)
    w_ext = (jnp.pad(w_in, ((0, 0), (0, o - w_in.shape[1]))) * col_scale).astype(BF16)
    bf_pad = jnp.pad(b_f.astype(F32), (0, LANES - N_HEADS_FOX)).reshape(1, LANES)
    return w_ext, offs, bf_pad


def kernel(x, c, w_ada, b_ada, g_attn, w_in, b_f, sinks, w_out, g_mlp, w_up, conv_w, conv_b,
           w_down, g_final):
    b, s, d = x.shape
    d_ff = w_down.shape[0]
    assert w_up.shape == (d, 2 * d_ff) and w_out.shape == (d, d)

    mod3 = _ada(c, w_ada, b_ada).reshape(b, 6, d)
    w_ext, offs, bf_pad = _inproj_weights(w_in, b_f)
    qa, ka2, va2, qf, kf, vf = _inproj(x, mod3, g_attn, w_ext, bf_pad, offs, tm=1024)
    ya, yb = _attn(qa, ka2, va2, qf, kf, vf, sinks, tq=1024, tk=s)
    return _mlp(x, ya, yb, mod3, g_mlp, g_final, w_out.astype(BF16), w_up.astype(BF16),
                conv_w, conv_b.reshape(1, 2 * d_ff), w_down.astype(BF16), tm=512, tf=256)
```
